```python
import math
import jax, jax.numpy as jnp
from jax import lax
import numpy as np

D_MODEL = 4096
BATCH = 4
SEQ = 2048
DEPTH = 2

CHUNK = 64
Q_BLOCK = 128
N_MEM = 256

DIFF_WIDTH = D_MODEL // 2
CONV_WIDTH = D_MODEL // 4
MEM_WIDTH = D_MODEL // 4

DIFF_HEADS = 8
DIFF_HEAD_DIM = DIFF_WIDTH // (2 * DIFF_HEADS)
DIFF_V_DIM = 2 * DIFF_HEAD_DIM
MEM_HEADS = 4
MEM_HEAD_DIM = MEM_WIDTH // MEM_HEADS
CONV_K = 31
LN_EPS = 1e-5
RMS_EPS = 1e-5
LAMBDA_INIT_SCALE = 0.1

DEEPNORM_ALPHA = (2.0 * DEPTH) ** 0.25
DEEPNORM_BETA = (8.0 * DEPTH) ** -0.25

COL_Q_DIFF = DIFF_WIDTH
COL_K_DIFF = DIFF_WIDTH
COL_V_DIFF = DIFF_WIDTH
COL_G_DIFF = DIFF_WIDTH
COL_GLU = 2 * CONV_WIDTH
COL_G_CONV = CONV_WIDTH
COL_Q_MEM = MEM_WIDTH
COL_G_MEM = MEM_WIDTH
IN_COLS = (COL_Q_DIFF + COL_K_DIFF + COL_V_DIFF + COL_G_DIFF + COL_GLU + COL_G_CONV
           + COL_Q_MEM + COL_G_MEM)

kernel_name = "hybrid_diffattn_conformer_memxattn_deepnorm"


def layer_norm(x, g, b):
    xf = x.astype(jnp.float32)
    mu = jnp.mean(xf, axis=-1, keepdims=True)
    var = jnp.mean(jnp.square(xf - mu), axis=-1, keepdims=True)
    y = (xf - mu) * lax.rsqrt(var + LN_EPS)
    return (y * g.astype(jnp.float32) + b.astype(jnp.float32)).astype(x.dtype)


def rms_norm(x, g):
    xf = x.astype(jnp.float32)
    y = xf * lax.rsqrt(jnp.mean(jnp.square(xf), axis=-1, keepdims=True) + RMS_EPS)
    return (y * g.astype(jnp.float32)).astype(x.dtype)


def split_cols(proj):
    sizes = [COL_Q_DIFF, COL_K_DIFF, COL_V_DIFF, COL_G_DIFF, COL_GLU, COL_G_CONV,
             COL_Q_MEM, COL_G_MEM]
    offs = []
    acc = 0
    for s in sizes[:-1]:
        acc += s
        offs.append(acc)
    return jnp.split(proj, offs, axis=-1)


def diff_attention(q, k, v, lq1, lk1, lq2, lk2, subln_g, lam_init):
    B, S = q.shape[0], q.shape[1]
    nb = S // Q_BLOCK
    lam = (jnp.exp(jnp.sum(lq1.astype(jnp.float32) * lk1.astype(jnp.float32)))
           - jnp.exp(jnp.sum(lq2.astype(jnp.float32) * lk2.astype(jnp.float32)))
           + lam_init)
    q = q * (DIFF_HEAD_DIM ** -0.5)
    qb = q.reshape(B, nb, Q_BLOCK, DIFF_HEADS, 2, DIFF_HEAD_DIM).transpose(1, 0, 2, 3, 4, 5)
    k_chunk = jnp.arange(S) // CHUNK

    def block(args):
        qi, i = args
        s = jnp.einsum('bqhmd,bkhmd->bhmqk', qi, k).astype(jnp.float32)
        q_chunk = (i * Q_BLOCK + jnp.arange(Q_BLOCK)) // CHUNK
        mask = k_chunk[None, :] <= q_chunk[:, None]
        s = jnp.where(mask, s, -jnp.inf)
        p = jax.nn.softmax(s, axis=-1)
        a = p[:, :, 0] - lam * p[:, :, 1]
        return jnp.einsum('bhqk,bkhe->bqhe', a.astype(v.dtype), v)

    o = lax.map(block, (qb, jnp.arange(nb)))
    o = o.transpose(1, 0, 2, 3, 4).reshape(B, S, DIFF_HEADS, DIFF_V_DIM)
    o = rms_norm(o, subln_g) * (1.0 - lam_init)
    return o.reshape(B, S, DIFF_WIDTH)


def conformer_conv(u, conv_w, conv_b, ln_g, ln_b, w_pw):
    a, gate = jnp.split(u, 2, axis=-1)
    h = a * jax.nn.sigmoid(gate)
    h = lax.conv_general_dilated(
        h, conv_w[:, None, :], window_strides=(1,), padding=[(CONV_K - 1, 0)],
        dimension_numbers=('NWC', 'WIO', 'NWC'), feature_group_count=CONV_WIDTH) + conv_b
    h = jax.nn.silu(layer_norm(h, ln_g, ln_b))
    return h @ w_pw


def memory_attention(q, mem, w_mem_kv):
    B, S = q.shape[0], q.shape[1]
    kv = mem @ w_mem_kv
    k, v = jnp.split(kv, 2, axis=-1)
    M = mem.shape[1]
    qh = q.reshape(B, S, MEM_HEADS, MEM_HEAD_DIM) * (MEM_HEAD_DIM ** -0.5)
    kh = k.reshape(B, M, MEM_HEADS, MEM_HEAD_DIM)
    vh = v.reshape(B, M, MEM_HEADS, MEM_HEAD_DIM)
    s = jnp.einsum('bshd,bmhd->bhsm', qh, kh).astype(jnp.float32)
    p = jax.nn.softmax(s, axis=-1).astype(v.dtype)
    o = jnp.einsum('bhsm,bmhd->bshd', p, vh)
    return o.reshape(B, S, MEM_WIDTH)


def setup_inputs(seed: int = 0) -> dict:
    key = jax.random.key(seed)
    ks = jax.random.split(key, 24)
    f32 = jnp.float32
    x = jax.random.normal(ks[0], (BATCH, SEQ, D_MODEL), f32)
    mem = jax.random.normal(ks[1], (BATCH, N_MEM, D_MODEL), f32)
    ln_in_g = 1.0 + 0.02 * jax.random.normal(ks[2], (D_MODEL,), f32)
    ln_in_b = 0.02 * jax.random.normal(ks[3], (D_MODEL,), f32)
    col_scale = jnp.concatenate([
        jnp.ones((COL_Q_DIFF + COL_K_DIFF,), f32),
        jnp.full((COL_V_DIFF,), DEEPNORM_BETA, f32),
        jnp.ones((COL_G_DIFF + COL_GLU + COL_G_CONV + COL_Q_MEM + COL_G_MEM,), f32)])
    w_in = (jax.random.normal(ks[4], (DEPTH, D_MODEL, IN_COLS), f32)
            * (D_MODEL ** -0.5) * col_scale)
    lambda_q1 = LAMBDA_INIT_SCALE * jax.random.normal(ks[5], (DEPTH, DIFF_HEAD_DIM), f32)
    lambda_k1 = LAMBDA_INIT_SCALE * jax.random.normal(ks[6], (DEPTH, DIFF_HEAD_DIM), f32)
    lambda_q2 = LAMBDA_INIT_SCALE * jax.random.normal(ks[7], (DEPTH, DIFF_HEAD_DIM), f32)
    lambda_k2 = LAMBDA_INIT_SCALE * jax.random.normal(ks[8], (DEPTH, DIFF_HEAD_DIM), f32)
    subln_g = 1.0 + 0.02 * jax.random.normal(ks[9], (DEPTH, DIFF_V_DIM), f32)
    conv_w = jax.random.normal(ks[10], (DEPTH, CONV_K, CONV_WIDTH), f32) * (CONV_K ** -0.5)
    conv_b = 0.02 * jax.random.normal(ks[11], (DEPTH, CONV_WIDTH), f32)
    conv_ln_g = 1.0 + 0.02 * jax.random.normal(ks[12], (DEPTH, CONV_WIDTH), f32)
    conv_ln_b = 0.02 * jax.random.normal(ks[13], (DEPTH, CONV_WIDTH), f32)
    w_conv_pw = (jax.random.normal(ks[14], (DEPTH, CONV_WIDTH, CONV_WIDTH), f32)
                 * (CONV_WIDTH ** -0.5) * DEEPNORM_BETA)
    mem_scale = jnp.concatenate([jnp.ones((MEM_WIDTH,), f32),
                                 jnp.full((MEM_WIDTH,), DEEPNORM_BETA, f32)])
    w_mem_kv = (jax.random.normal(ks[15], (DEPTH, D_MODEL, 2 * MEM_WIDTH), f32)
                * (D_MODEL ** -0.5) * mem_scale)
    w_out = (jax.random.normal(ks[16], (DEPTH, D_MODEL, D_MODEL), f32)
             * (D_MODEL ** -0.5) * DEEPNORM_BETA)
    ln_post_g = 1.0 + 0.02 * jax.random.normal(ks[17], (DEPTH, D_MODEL), f32)
    ln_post_b = 0.02 * jax.random.normal(ks[18], (DEPTH, D_MODEL), f32)
    return {"x": x, "mem": mem, "ln_in_g": ln_in_g, "ln_in_b": ln_in_b, "w_in": w_in,
            "lambda_q1": lambda_q1, "lambda_k1": lambda_k1, "lambda_q2": lambda_q2,
            "lambda_k2": lambda_k2, "subln_g": subln_g, "conv_w": conv_w, "conv_b": conv_b,
            "conv_ln_g": conv_ln_g, "conv_ln_b": conv_ln_b, "w_conv_pw": w_conv_pw,
            "w_mem_kv": w_mem_kv, "w_out": w_out, "ln_post_g": ln_post_g,
            "ln_post_b": ln_post_b}


def reference(x, mem, ln_in_g, ln_in_b, w_in, lambda_q1, lambda_k1, lambda_q2, lambda_k2,
              subln_g, conv_w, conv_b, conv_ln_g, conv_ln_b, w_conv_pw, w_mem_kv, w_out,
              ln_post_g, ln_post_b):
    B, S = x.shape[0], x.shape[1]
    h = layer_norm(x, ln_in_g, ln_in_b)
    for l in range(DEPTH):
        lam_init = 0.8 - 0.6 * math.exp(-0.3 * l)
        proj = h @ w_in[l]
        q_d, k_d, v_d, g_d, glu_in, g_c, q_m, g_m = split_cols(proj)
        q_d = q_d.reshape(B, S, DIFF_HEADS, 2, DIFF_HEAD_DIM)
        k_d = k_d.reshape(B, S, DIFF_HEADS, 2, DIFF_HEAD_DIM)
        v_d = v_d.reshape(B, S, DIFF_HEADS, DIFF_V_DIM)
        y_diff = diff_attention(q_d, k_d, v_d, lambda_q1[l], lambda_k1[l], lambda_q2[l],
                                lambda_k2[l], subln_g[l], lam_init)
        y_conv = conformer_conv(glu_in, conv_w[l], conv_b[l], conv_ln_g[l], conv_ln_b[l],
                                w_conv_pw[l])
        y_mem = memory_attention(q_m, mem, w_mem_kv[l])
        y = jnp.concatenate([y_diff * jax.nn.silu(g_d),
                             y_conv * jax.nn.silu(g_c),
                             y_mem * jax.nn.silu(g_m)], axis=-1) @ w_out[l]
        h = layer_norm(DEEPNORM_ALPHA * h + y, ln_post_g[l], ln_post_b[l])
    return h
```

```python
import functools
import math

import jax
import jax.numpy as jnp
from jax import lax
from jax.experimental import pallas as pl
from jax.experimental.pallas import tpu as pltpu

D_MODEL = 4096
DEPTH = 2
CHUNK = 64
DIFF_WIDTH = D_MODEL // 2
CONV_WIDTH = D_MODEL // 4
MEM_WIDTH = D_MODEL // 4
DIFF_HEADS = 8
DIFF_HEAD_DIM = DIFF_WIDTH // (2 * DIFF_HEADS)
DIFF_V_DIM = 2 * DIFF_HEAD_DIM
MEM_HEADS = 4
MEM_HEAD_DIM = MEM_WIDTH // MEM_HEADS
CONV_K = 31
LN_EPS = 1e-5
RMS_EPS = 1e-5
DEEPNORM_ALPHA = (2.0 * DEPTH) ** 0.25

OFF_Q = 0
OFF_K = OFF_Q + DIFF_WIDTH
OFF_V = OFF_K + DIFF_WIDTH
OFF_GD = OFF_V + DIFF_WIDTH
OFF_GLU_A = OFF_GD + DIFF_WIDTH
OFF_GLU_G = OFF_GLU_A + CONV_WIDTH
OFF_GC = OFF_GLU_G + CONV_WIDTH
OFF_QM = OFF_GC + CONV_WIDTH
OFF_GM = OFF_QM + MEM_WIDTH
IN_COLS = OFF_GM + MEM_WIDTH

LOG2E = 1.4426950408889634
Q_DIFF_SCALE = DIFF_HEAD_DIM ** -0.5 * LOG2E

SUBLANES = 8
CONV_HALO = 32
ATTN_TQ = 256
MIB = 1024 * 1024

F32 = jnp.float32
BF16 = jnp.bfloat16


def _cparams(sem, vmem_mib):
    return pltpu.CompilerParams(dimension_semantics=sem, vmem_limit_bytes=int(vmem_mib * MIB))


def _silu(x):
    return x * jax.nn.sigmoid(x)


def _ln_rows(x, g, b):
    mu = jnp.mean(x, axis=-1, keepdims=True)
    xc = x - mu
    var = jnp.mean(xc * xc, axis=-1, keepdims=True)
    return xc * lax.rsqrt(var + LN_EPS) * g + b


def _ln_kernel(x_ref, g_ref, b_ref, of_ref, ob_ref):
    y = _ln_rows(x_ref[...], g_ref[...], b_ref[...])
    of_ref[...] = y
    ob_ref[...] = y.astype(BF16)


def _layer_norm(x, g, b, *, tm=256):
    m, d = x.shape
    return pl.pallas_call(
        _ln_kernel,
        grid=(m // tm,),
        in_specs=[pl.BlockSpec((tm, d), lambda i: (i, 0)),
                  pl.BlockSpec((1, d), lambda i: (0, 0)),
                  pl.BlockSpec((1, d), lambda i: (0, 0))],
        out_specs=[pl.BlockSpec((tm, d), lambda i: (i, 0)),
                   pl.BlockSpec((tm, d), lambda i: (i, 0))],
        out_shape=[jax.ShapeDtypeStruct((m, d), F32), jax.ShapeDtypeStruct((m, d), BF16)],
        compiler_params=_cparams(("arbitrary",), 40),
        name="layer_norm",
    )(x, g.reshape(1, d), b.reshape(1, d))


def _matmul_kernel(a_ref, w_ref, o_ref, wb_ref, *, n_scaled_tiles, scale):
    j = pl.program_id(0)

    @pl.when(pl.program_id(1) == 0)
    def _():
        wb_ref[...] = w_ref[...].astype(BF16)

    acc = jnp.dot(a_ref[...].astype(BF16), wb_ref[...], preferred_element_type=F32)
    if n_scaled_tiles:
        acc = acc * jnp.where(j < n_scaled_tiles, scale, 1.0).astype(F32)
    o_ref[...] = acc.astype(o_ref.dtype)


def _matmul_ws(a, w, layer, *, tm, tn, n_scaled_tiles=0, scale=1.0, vmem_mib=56, name="matmul"):
    m, k = a.shape
    n = w.shape[-1]
    return pl.pallas_call(
        functools.partial(_matmul_kernel, n_scaled_tiles=n_scaled_tiles, scale=scale),
        grid=(n // tn, m // tm),
        in_specs=[pl.BlockSpec((tm, k), lambda j, i: (i, 0)),
                  pl.BlockSpec((None, k, tn), lambda j, i: (layer, 0, j))],
        out_specs=pl.BlockSpec((tm, tn), lambda j, i: (i, j)),
        out_shape=jax.ShapeDtypeStruct((m, n), BF16),
        scratch_shapes=[pltpu.VMEM((k, tn), BF16)],
        compiler_params=_cparams(("arbitrary", "arbitrary"), vmem_mib),
        name=name,
    )(a, w)


def _softmax_pv(q, k_ref, v_ref, kv_lo, col, mask):
    nt = (((1,), (1,)), ((), ()))
    k_d = k_ref[kv_lo:kv_lo + ATTN_TQ, col:col + DIFF_HEAD_DIM]
    s_d = lax.dot_general(q, k_d, nt, preferred_element_type=F32)
    s_d = jnp.where(mask, s_d, -jnp.inf)
    m = jnp.max(s_d, axis=-1, keepdims=True)
    if kv_lo:
        k_o = k_ref[0:kv_lo, col:col + DIFF_HEAD_DIM]
        s_o = lax.dot_general(q, k_o, nt, preferred_element_type=F32)
        m = jnp.maximum(m, jnp.max(s_o, axis=-1, keepdims=True))
    p_d = jnp.exp2(s_d - m)
    l = jnp.sum(p_d, axis=-1, keepdims=True)
    o = jnp.dot(p_d.astype(BF16), v_ref[kv_lo:kv_lo + ATTN_TQ, :], preferred_element_type=F32)
    if kv_lo:
        p_o = jnp.exp2(s_o - m)
        l = l + jnp.sum(p_o, axis=-1, keepdims=True)
        o = o + jnp.dot(p_o.astype(BF16), v_ref[0:kv_lo, :], preferred_element_type=F32)
    return o * (1.0 / l)


def _diff_attn_kernel(q_ref, k_ref, v_ref, g_ref, lq1_ref, lk1_ref, lq2_ref, lk2_ref, sg_ref,
                      o_ref, *, lam_init, seq):
    lam = (jnp.exp(jnp.sum(lq1_ref[...] * lk1_ref[...], axis=-1, keepdims=True))
           - jnp.exp(jnp.sum(lq2_ref[...] * lk2_ref[...], axis=-1, keepdims=True))
           + lam_init)
    row_chunk = lax.broadcasted_iota(jnp.int32, (ATTN_TQ, ATTN_TQ), 0) // CHUNK
    col_chunk = lax.broadcasted_iota(jnp.int32, (ATTN_TQ, ATTN_TQ), 1) // CHUNK
    mask = col_chunk <= row_chunk
    out_gain = sg_ref[...] * (1.0 - lam_init)
    for t in range(seq // ATTN_TQ):
        lo = t * ATTN_TQ
        rows = slice(lo, lo + ATTN_TQ)
        o1 = _softmax_pv(q_ref[rows, 0:DIFF_HEAD_DIM], k_ref, v_ref, lo, 0, mask)
        o2 = _softmax_pv(q_ref[rows, DIFF_HEAD_DIM:DIFF_V_DIM], k_ref, v_ref, lo, DIFF_HEAD_DIM, mask)
        o = o1 - lam * o2
        o = o * lax.rsqrt(jnp.mean(o * o, axis=-1, keepdims=True) + RMS_EPS) * out_gain
        o_ref[rows, :] = (o * _silu(g_ref[rows, :].astype(F32))).astype(o_ref.dtype)


def _diff_attention(proj, lq1, lk1, lq2, lk2, subln_g, layer, *, batch, seq, lam_init):
    w = DIFF_V_DIM
    head_block = lambda off: pl.BlockSpec((seq, w), lambda b, h: (b, off // w + h))
    lam_spec = pl.BlockSpec((None, 1, DIFF_HEAD_DIM), lambda b, h: (layer, 0, 0))
    return pl.pallas_call(
        functools.partial(_diff_attn_kernel, lam_init=lam_init, seq=seq),
        grid=(batch, DIFF_HEADS),
        in_specs=[head_block(OFF_Q), head_block(OFF_K), head_block(OFF_V), head_block(OFF_GD),
                  lam_spec, lam_spec, lam_spec, lam_spec,
                  pl.BlockSpec((None, 1, DIFF_V_DIM), lambda b, h: (layer, 0, 0))],
        out_specs=pl.BlockSpec((seq, w), lambda b, h: (b, h)),
        out_shape=jax.ShapeDtypeStruct((batch * seq, DIFF_WIDTH), BF16),
        compiler_params=_cparams(("arbitrary", "arbitrary"), 48),
        name="diff_attention",
    )(proj, proj, proj, proj,
      lq1.reshape(DEPTH, 1, -1), lk1.reshape(DEPTH, 1, -1), lq2.reshape(DEPTH, 1, -1),
      lk2.reshape(DEPTH, 1, -1), subln_g.reshape(DEPTH, 1, -1))


CONV_TS = 256
CONV_RC = 32
CONV_PAD = CONV_HALO - (CONV_K - 1)


def _conv_kernel(a_ref, gt_ref, ah_ref, gh_ref, gc_ref, cw_ref, cb_ref, lg_ref, lb_ref, wpw_ref,
                 o_ref, hbuf_ref, cout_ref, wtap_ref, wpw_bf_ref, *, tiles_per_seq):
    i = pl.program_id(0)

    @pl.when(i == 0)
    def _():
        wpw_bf_ref[...] = wpw_ref[...].astype(BF16)
        for j in range(CONV_K):
            wtap_ref[j * SUBLANES:(j + 1) * SUBLANES, :] = jnp.broadcast_to(
                cw_ref[j:j + 1, :], (SUBLANES, CONV_WIDTH))

    halo = ah_ref[...].astype(F32) * jax.nn.sigmoid(gh_ref[...].astype(F32))
    keep = (i % tiles_per_seq != 0).astype(F32)
    hbuf_ref[0:CONV_HALO, :] = halo * keep
    hbuf_ref[CONV_HALO:CONV_HALO + CONV_TS, :] = (
        a_ref[...].astype(F32) * jax.nn.sigmoid(gt_ref[...].astype(F32)))

    n_sub = CONV_RC // SUBLANES
    for r in range(CONV_TS // CONV_RC):
        acc = [jnp.zeros((SUBLANES, CONV_WIDTH), F32) for _ in range(n_sub)]
        for j in range(CONV_K):
            wj = wtap_ref[j * SUBLANES:(j + 1) * SUBLANES, :]
            for s in range(n_sub):
                start = CONV_PAD + r * CONV_RC + s * SUBLANES + j
                acc[s] = acc[s] + wj * hbuf_ref[start:start + SUBLANES, :]
        for s in range(n_sub):
            row = r * CONV_RC + s * SUBLANES
            cout_ref[row:row + SUBLANES, :] = acc[s]

    c = cout_ref[...] + cb_ref[...]
    c = _silu(_ln_rows(c, lg_ref[...], lb_ref[...]))
    y = jnp.dot(c.astype(BF16), wpw_bf_ref[...], preferred_element_type=F32)
    o_ref[...] = (y * _silu(gc_ref[...].astype(F32))).astype(o_ref.dtype)


def _conv_branch(proj, conv_w, conv_b, ln_g, ln_b, w_pw, layer, *, seq):
    m = proj.shape[0]
    cw = CONV_WIDTH
    halo_per_tile = CONV_TS // CONV_HALO
    main = lambda off: pl.BlockSpec((CONV_TS, cw), lambda i: (i, off // cw))
    halo = lambda off: pl.BlockSpec(
        (CONV_HALO, cw), lambda i: (jnp.maximum(i * halo_per_tile - 1, 0), off // cw))
    vec = pl.BlockSpec((None, 1, cw), lambda i: (layer, 0, 0))
    return pl.pallas_call(
        functools.partial(_conv_kernel, tiles_per_seq=seq // CONV_TS),
        grid=(m // CONV_TS,),
        in_specs=[main(OFF_GLU_A), main(OFF_GLU_G), halo(OFF_GLU_A), halo(OFF_GLU_G), main(OFF_GC),
                  pl.BlockSpec((None, CONV_K, cw), lambda i: (layer, 0, 0)),
                  vec, vec, vec,
                  pl.BlockSpec((None, cw, cw), lambda i: (layer, 0, 0))],
        out_specs=pl.BlockSpec((CONV_TS, cw), lambda i: (i, 0)),
        out_shape=jax.ShapeDtypeStruct((m, cw), BF16),
        scratch_shapes=[pltpu.VMEM((CONV_HALO + CONV_TS, cw), F32),
                        pltpu.VMEM((CONV_TS, cw), F32),
                        pltpu.VMEM((CONV_K * SUBLANES, cw), F32),
                        pltpu.VMEM((cw, cw), BF16)],
        compiler_params=_cparams(("arbitrary",), 40),
        name="conv_branch",
    )(proj, proj, proj, proj, proj, conv_w, conv_b.reshape(DEPTH, 1, cw),
      ln_g.reshape(DEPTH, 1, cw), ln_b.reshape(DEPTH, 1, cw), w_pw)


MEM_TQ = 512


def _mem_attn_kernel(q_ref, k_ref, v_ref, g_ref, o_ref):
    nt = (((1,), (1,)), ((), ()))
    s = lax.dot_general(q_ref[...], k_ref[...], nt, preferred_element_type=F32)
    s = s * (MEM_HEAD_DIM ** -0.5)
    p = jnp.exp(s - jnp.max(s, axis=-1, keepdims=True))
    l = jnp.sum(p, axis=-1, keepdims=True)
    o = jnp.dot(p.astype(BF16), v_ref[...], preferred_element_type=F32) / l
    o_ref[...] = (o * _silu(g_ref[...].astype(F32))).astype(o_ref.dtype)


def _mem_attention(proj, kv, *, batch, seq, n_mem):
    w = MEM_HEAD_DIM
    tq = seq // MEM_TQ
    row = lambda b, h, t: b * tq + t
    return pl.pallas_call(
        _mem_attn_kernel,
        grid=(batch, MEM_HEADS, tq),
        in_specs=[pl.BlockSpec((MEM_TQ, w), lambda b, h, t: (row(b, h, t), OFF_QM // w + h)),
                  pl.BlockSpec((n_mem, w), lambda b, h, t: (b, h)),
                  pl.BlockSpec((n_mem, w), lambda b, h, t: (b, MEM_HEADS + h)),
                  pl.BlockSpec((MEM_TQ, w), lambda b, h, t: (row(b, h, t), OFF_GM // w + h))],
        out_specs=pl.BlockSpec((MEM_TQ, w), lambda b, h, t: (row(b, h, t), h)),
        out_shape=jax.ShapeDtypeStruct((batch * seq, MEM_WIDTH), BF16),
        compiler_params=_cparams(("arbitrary", "arbitrary", "arbitrary"), 32),
        name="mem_attention",
    )(proj, kv, kv, proj)


def _out_proj_kernel(yd_ref, yc_ref, ym_ref, w_ref, h_ref, z_ref, wb_ref):
    @pl.when(pl.program_id(1) == 0)
    def _():
        wb_ref[...] = w_ref[...].astype(BF16)

    c0, c1 = DIFF_WIDTH, DIFF_WIDTH + CONV_WIDTH
    acc = jnp.dot(yd_ref[...], wb_ref[0:c0, :], preferred_element_type=F32)
    acc = acc + jnp.dot(yc_ref[...], wb_ref[c0:c1, :], preferred_element_type=F32)
    acc = acc + jnp.dot(ym_ref[...], wb_ref[c1:D_MODEL, :], preferred_element_type=F32)
    z_ref[...] = DEEPNORM_ALPHA * h_ref[...] + acc


def _out_proj(yd, yc, ym, w_out, h, layer, *, tm=512, tn=512):
    m = h.shape[0]
    return pl.pallas_call(
        _out_proj_kernel,
        grid=(D_MODEL // tn, m // tm),
        in_specs=[pl.BlockSpec((tm, DIFF_WIDTH), lambda j, i: (i, 0)),
                  pl.BlockSpec((tm, CONV_WIDTH), lambda j, i: (i, 0)),
                  pl.BlockSpec((tm, MEM_WIDTH), lambda j, i: (i, 0)),
                  pl.BlockSpec((None, D_MODEL, tn), lambda j, i: (layer, 0, j)),
                  pl.BlockSpec((tm, tn), lambda j, i: (i, j))],
        out_specs=pl.BlockSpec((tm, tn), lambda j, i: (i, j)),
        out_shape=jax.ShapeDtypeStruct((m, D_MODEL), F32),
        scratch_shapes=[pltpu.VMEM((D_MODEL, tn), BF16)],
        compiler_params=_cparams(("arbitrary", "arbitrary"), 48),
        name="out_proj",
    )(yd, yc, ym, w_out, h)


def kernel(x, mem, ln_in_g, ln_in_b, w_in, lambda_q1, lambda_k1, lambda_q2, lambda_k2, subln_g,
           conv_w, conv_b, conv_ln_g, conv_ln_b, w_conv_pw, w_mem_kv, w_out, ln_post_g, ln_post_b):
    batch, seq, d = x.shape
    n_mem = mem.shape[1]
    assert d == D_MODEL and seq % ATTN_TQ == 0 and seq % CONV_TS == 0 and seq % MEM_TQ == 0
    h_f32, h_bf16 = _layer_norm(x.reshape(batch * seq, d), ln_in_g, ln_in_b)
    mem2 = mem.reshape(batch * n_mem, d)
    for layer in range(DEPTH):
        lam_init = 0.8 - 0.6 * math.exp(-0.3 * layer)
        proj = _matmul_ws(h_bf16, w_in, layer, tm=512, tn=1024,
                          n_scaled_tiles=DIFF_WIDTH // 1024, scale=Q_DIFF_SCALE, name="in_proj")
        kv = _matmul_ws(mem2, w_mem_kv, layer, tm=256, tn=512, vmem_mib=48, name="mem_kv")
        yd = _diff_attention(proj, lambda_q1, lambda_k1, lambda_q2, lambda_k2, subln_g, layer,
                             batch=batch, seq=seq, lam_init=lam_init)
        yc = _conv_branch(proj, conv_w, conv_b, conv_ln_g, conv_ln_b, w_conv_pw, layer, seq=seq)
        ym = _mem_attention(proj, kv, batch=batch, seq=seq, n_mem=n_mem)
        z = _out_proj(yd, yc, ym, w_out, h_f32, layer)
        h_f32, h_bf16 = _layer_norm(z, ln_post_g[layer], ln_post_b[layer])
    return h_f32.reshape(batch, seq, d)
```

```python
import functools
import math

import jax
import jax.numpy as jnp
from jax import lax
from jax.experimental import pallas as pl
from jax.experimental.pallas import tpu as pltpu

D_MODEL = 4096
DEPTH = 2
CHUNK = 64
DIFF_WIDTH = D_MODEL // 2
CONV_WIDTH = D_MODEL // 4
MEM_WIDTH = D_MODEL // 4
DIFF_HEADS = 8
DIFF_HEAD_DIM = DIFF_WIDTH // (2 * DIFF_HEADS)
DIFF_V_DIM = 2 * DIFF_HEAD_DIM
MEM_HEADS = 4
MEM_HEAD_DIM = MEM_WIDTH // MEM_HEADS
CONV_K = 31
LN_EPS = 1e-5
RMS_EPS = 1e-5
DEEPNORM_ALPHA = (2.0 * DEPTH) ** 0.25

OFF_Q = 0
OFF_K = OFF_Q + DIFF_WIDTH
OFF_V = OFF_K + DIFF_WIDTH
OFF_GD = OFF_V + DIFF_WIDTH
OFF_GLU_A = OFF_GD + DIFF_WIDTH
OFF_GLU_G = OFF_GLU_A + CONV_WIDTH
OFF_GC = OFF_GLU_G + CONV_WIDTH
OFF_QM = OFF_GC + CONV_WIDTH
OFF_GM = OFF_QM + MEM_WIDTH
IN_COLS = OFF_GM + MEM_WIDTH

LOG2E = 1.4426950408889634
Q_DIFF_SCALE = DIFF_HEAD_DIM ** -0.5 * LOG2E

LANES = 128
SUBLANES = 8
CONV_HALO = 32
ATTN_TQ = 256
MIB = 1024 * 1024

F32 = jnp.float32
BF16 = jnp.bfloat16


def _cparams(sem, vmem_mib):
    return pltpu.CompilerParams(dimension_semantics=sem, vmem_limit_bytes=int(vmem_mib * MIB))


def _silu(x):
    return x * jax.nn.sigmoid(x)


def _lane_tile(v, width):
    return jnp.concatenate([v] * (width // LANES), axis=1)


def _row_stats(x):
    mu = jnp.mean(x, axis=-1, keepdims=True)
    xc = x - mu
    var = jnp.mean(xc * xc, axis=-1, keepdims=True)
    return mu, xc, lax.rsqrt(var + LN_EPS)


def _ln_rows(x, g, b):
    _, xc, rstd = _row_stats(x)
    return xc * rstd * g + b


def _ln_stats_kernel(x_ref, g_ref, b_ref, hb_ref, mu_ref, rstd_ref):
    mu, xc, rstd = _row_stats(x_ref[...])
    hb_ref[...] = (xc * rstd * g_ref[...] + b_ref[...]).astype(BF16)
    mu_ref[...] = jnp.broadcast_to(mu, mu_ref.shape)
    rstd_ref[...] = jnp.broadcast_to(rstd, rstd_ref.shape)


def _ln_final_kernel(x_ref, g_ref, b_ref, o_ref):
    o_ref[...] = _ln_rows(x_ref[...], g_ref[...], b_ref[...])


def _layer_norm(x, g, b, *, final, tm=256):
    m, d = x.shape
    row = pl.BlockSpec((tm, d), lambda i: (i, 0))
    vec = pl.BlockSpec((1, d), lambda i: (0, 0))
    stat = pl.BlockSpec((tm, LANES), lambda i: (i, 0))
    if final:
        out_specs, out_shape, body = row, jax.ShapeDtypeStruct((m, d), F32), _ln_final_kernel
    else:
        out_specs = [row, stat, stat]
        out_shape = [jax.ShapeDtypeStruct((m, d), BF16), jax.ShapeDtypeStruct((m, LANES), F32),
                     jax.ShapeDtypeStruct((m, LANES), F32)]
        body = _ln_stats_kernel
    return pl.pallas_call(
        body, grid=(m // tm,), in_specs=[row, vec, vec], out_specs=out_specs, out_shape=out_shape,
        compiler_params=_cparams(("arbitrary",), 40),
        name="layer_norm_final" if final else "layer_norm",
    )(x, g.reshape(1, d), b.reshape(1, d))


def _matmul_kernel(a_ref, w_ref, o_ref, wb_ref, *, n_scaled_tiles, scale):
    j = pl.program_id(0)

    @pl.when(pl.program_id(1) == 0)
    def _():
        wb_ref[...] = w_ref[...].astype(BF16)

    acc = jnp.dot(a_ref[...].astype(BF16), wb_ref[...], preferred_element_type=F32)
    if n_scaled_tiles:
        acc = acc * jnp.where(j < n_scaled_tiles, scale, 1.0).astype(F32)
    o_ref[...] = acc.astype(o_ref.dtype)


def _matmul_ws(a, w, layer, *, tm, tn, n_scaled_tiles=0, scale=1.0, vmem_mib=56, name="matmul"):
    m, k = a.shape
    n = w.shape[-1]
    return pl.pallas_call(
        functools.partial(_matmul_kernel, n_scaled_tiles=n_scaled_tiles, scale=scale),
        grid=(n // tn, m // tm),
        in_specs=[pl.BlockSpec((tm, k), lambda j, i: (i, 0)),
                  pl.BlockSpec((None, k, tn), lambda j, i: (layer, 0, j))],
        out_specs=pl.BlockSpec((tm, tn), lambda j, i: (i, j)),
        out_shape=jax.ShapeDtypeStruct((m, n), BF16),
        scratch_shapes=[pltpu.VMEM((k, tn), BF16)],
        compiler_params=_cparams(("arbitrary", "arbitrary"), vmem_mib),
        name=name,
    )(a, w)


_NT_DIMS = (((1,), (1,)), ((), ()))


def _diff_attn_kernel(q_ref, k_ref, v_ref, g_ref, lq1_ref, lk1_ref, lq2_ref, lk2_ref, sg_ref,
                      o_ref, s_ref, *, lam_init, seq):
    d = DIFF_HEAD_DIM
    lam = (jnp.exp(jnp.sum(lq1_ref[...] * lk1_ref[...], axis=-1, keepdims=True))
           - jnp.exp(jnp.sum(lq2_ref[...] * lk2_ref[...], axis=-1, keepdims=True))
           + lam_init)
    row_chunk = lax.broadcasted_iota(jnp.int32, (ATTN_TQ, ATTN_TQ), 0) // CHUNK
    col_chunk = lax.broadcasted_iota(jnp.int32, (ATTN_TQ, ATTN_TQ), 1) // CHUNK
    mask = col_chunk <= row_chunk
    out_gain = sg_ref[...] * (1.0 - lam_init)
    row_max = {}

    def scores(t):
        lo = t * ATTN_TQ
        for m in range(2):
            cols = slice(m * d, (m + 1) * d)
            q = q_ref[lo:lo + ATTN_TQ, cols]
            s_d = lax.dot_general(q, k_ref[lo:lo + ATTN_TQ, cols], _NT_DIMS, preferred_element_type=F32)
            s_d = jnp.where(mask, s_d, -jnp.inf)
            s_ref[t % 2, m, :, lo:lo + ATTN_TQ] = s_d
            mx = jnp.max(s_d, axis=-1, keepdims=True)
            if lo:
                s_o = lax.dot_general(q, k_ref[0:lo, cols], _NT_DIMS, preferred_element_type=F32)
                s_ref[t % 2, m, :, 0:lo] = s_o
                mx = jnp.maximum(mx, jnp.max(s_o, axis=-1, keepdims=True))
            row_max[(t, m)] = mx

    def outputs(t):
        lo = t * ATTN_TQ
        kv = lo + ATTN_TQ
        o = []
        for m in range(2):
            p = jnp.exp2(s_ref[t % 2, m, :, 0:kv] - row_max[(t, m)])
            l = jnp.sum(p, axis=-1, keepdims=True)
            pv = jnp.dot(p.astype(BF16), v_ref[0:kv, :], preferred_element_type=F32)
            o.append(pv * (1.0 / l))
        o = o[0] - lam * o[1]
        o = o * lax.rsqrt(jnp.mean(o * o, axis=-1, keepdims=True) + RMS_EPS) * out_gain
        o_ref[lo:lo + ATTN_TQ, :] = (o * _silu(g_ref[lo:lo + ATTN_TQ, :].astype(F32))).astype(o_ref.dtype)

    n_tiles = seq // ATTN_TQ
    for t in range(n_tiles + 1):
        if t < n_tiles:
            scores(t)
        if t >= 1:
            outputs(t - 1)


def _diff_attention(proj, lq1, lk1, lq2, lk2, subln_g, layer, *, batch, seq, lam_init):
    w = DIFF_V_DIM
    head_block = lambda off: pl.BlockSpec((seq, w), lambda b, h: (b, off // w + h))
    lam_spec = pl.BlockSpec((None, 1, DIFF_HEAD_DIM), lambda b, h: (layer, 0, 0))
    return pl.pallas_call(
        functools.partial(_diff_attn_kernel, lam_init=lam_init, seq=seq),
        grid=(batch, DIFF_HEADS),
        in_specs=[head_block(OFF_Q), head_block(OFF_K), head_block(OFF_V), head_block(OFF_GD),
                  lam_spec, lam_spec, lam_spec, lam_spec,
                  pl.BlockSpec((None, 1, DIFF_V_DIM), lambda b, h: (layer, 0, 0))],
        out_specs=pl.BlockSpec((seq, w), lambda b, h: (b, h)),
        out_shape=jax.ShapeDtypeStruct((batch * seq, DIFF_WIDTH), BF16),
        scratch_shapes=[pltpu.VMEM((2, 2, ATTN_TQ, seq), F32)],
        compiler_params=_cparams(("arbitrary", "arbitrary"), 48),
        name="diff_attention",
    )(proj, proj, proj, proj,
      lq1.reshape(DEPTH, 1, -1), lk1.reshape(DEPTH, 1, -1), lq2.reshape(DEPTH, 1, -1),
      lk2.reshape(DEPTH, 1, -1), subln_g.reshape(DEPTH, 1, -1))


CONV_TS = 256
CONV_RC = 32
CONV_PAD = CONV_HALO - (CONV_K - 1)


def _conv_kernel(a_ref, gt_ref, ah_ref, gh_ref, gc_ref, cw_ref, cb_ref, lg_ref, lb_ref, wpw_ref,
                 o_ref, hbuf_ref, cout_ref, wtap_ref, wpw_bf_ref, *, tiles_per_seq):
    i = pl.program_id(0)

    @pl.when(i == 0)
    def _():
        wpw_bf_ref[...] = wpw_ref[...].astype(BF16)
        for j in range(CONV_K):
            wtap_ref[j * SUBLANES:(j + 1) * SUBLANES, :] = jnp.broadcast_to(
                cw_ref[j:j + 1, :], (SUBLANES, CONV_WIDTH))

    halo = ah_ref[...].astype(F32) * jax.nn.sigmoid(gh_ref[...].astype(F32))
    keep = (i % tiles_per_seq != 0).astype(F32)
    hbuf_ref[0:CONV_HALO, :] = halo * keep
    hbuf_ref[CONV_HALO:CONV_HALO + CONV_TS, :] = (
        a_ref[...].astype(F32) * jax.nn.sigmoid(gt_ref[...].astype(F32)))

    n_sub = CONV_RC // SUBLANES
    for r in range(CONV_TS // CONV_RC):
        acc = [jnp.zeros((SUBLANES, CONV_WIDTH), F32) for _ in range(n_sub)]
        for j in range(CONV_K):
            wj = wtap_ref[j * SUBLANES:(j + 1) * SUBLANES, :]
            for s in range(n_sub):
                start = CONV_PAD + r * CONV_RC + s * SUBLANES + j
                acc[s] = acc[s] + wj * hbuf_ref[start:start + SUBLANES, :]
        for s in range(n_sub):
            row = r * CONV_RC + s * SUBLANES
            cout_ref[row:row + SUBLANES, :] = acc[s]

    c = cout_ref[...] + cb_ref[...]
    c = _silu(_ln_rows(c, lg_ref[...], lb_ref[...]))
    y = jnp.dot(c.astype(BF16), wpw_bf_ref[...], preferred_element_type=F32)
    o_ref[...] = (y * _silu(gc_ref[...].astype(F32))).astype(o_ref.dtype)


def _conv_branch(proj, conv_w, conv_b, ln_g, ln_b, w_pw, layer, *, seq):
    m = proj.shape[0]
    cw = CONV_WIDTH
    halo_per_tile = CONV_TS // CONV_HALO
    main = lambda off: pl.BlockSpec((CONV_TS, cw), lambda i: (i, off // cw))
    halo = lambda off: pl.BlockSpec(
        (CONV_HALO, cw), lambda i: (jnp.maximum(i * halo_per_tile - 1, 0), off // cw))
    vec = pl.BlockSpec((None, 1, cw), lambda i: (layer, 0, 0))
    return pl.pallas_call(
        functools.partial(_conv_kernel, tiles_per_seq=seq // CONV_TS),
        grid=(m // CONV_TS,),
        in_specs=[main(OFF_GLU_A), main(OFF_GLU_G), halo(OFF_GLU_A), halo(OFF_GLU_G), main(OFF_GC),
                  pl.BlockSpec((None, CONV_K, cw), lambda i: (layer, 0, 0)),
                  vec, vec, vec,
                  pl.BlockSpec((None, cw, cw), lambda i: (layer, 0, 0))],
        out_specs=pl.BlockSpec((CONV_TS, cw), lambda i: (i, 0)),
        out_shape=jax.ShapeDtypeStruct((m, cw), BF16),
        scratch_shapes=[pltpu.VMEM((CONV_HALO + CONV_TS, cw), F32),
                        pltpu.VMEM((CONV_TS, cw), F32),
                        pltpu.VMEM((CONV_K * SUBLANES, cw), F32),
                        pltpu.VMEM((cw, cw), BF16)],
        compiler_params=_cparams(("arbitrary",), 40),
        name="conv_branch",
    )(proj, proj, proj, proj, proj, conv_w, conv_b.reshape(DEPTH, 1, cw),
      ln_g.reshape(DEPTH, 1, cw), ln_b.reshape(DEPTH, 1, cw), w_pw)


MEM_TQ = 512


def _mem_attn_kernel(q_ref, k_ref, v_ref, g_ref, o_ref):
    for h in range(MEM_HEADS):
        cols = slice(h * MEM_HEAD_DIM, (h + 1) * MEM_HEAD_DIM)
        s = lax.dot_general(q_ref[:, cols], k_ref[:, cols], _NT_DIMS, preferred_element_type=F32)
        s = s * (MEM_HEAD_DIM ** -0.5)
        p = jnp.exp(s - jnp.max(s, axis=-1, keepdims=True))
        l = jnp.sum(p, axis=-1, keepdims=True)
        o = jnp.dot(p.astype(BF16), v_ref[:, cols], preferred_element_type=F32) * (1.0 / l)
        o_ref[:, cols] = (o * _silu(g_ref[:, cols].astype(F32))).astype(o_ref.dtype)


def _mem_attention(proj, kv, *, batch, seq, n_mem):
    w = MEM_WIDTH
    tq = seq // MEM_TQ
    return pl.pallas_call(
        _mem_attn_kernel,
        grid=(batch, tq),
        in_specs=[pl.BlockSpec((MEM_TQ, w), lambda b, t: (b * tq + t, OFF_QM // w)),
                  pl.BlockSpec((n_mem, w), lambda b, t: (b, 0)),
                  pl.BlockSpec((n_mem, w), lambda b, t: (b, 1)),
                  pl.BlockSpec((MEM_TQ, w), lambda b, t: (b * tq + t, OFF_GM // w))],
        out_specs=pl.BlockSpec((MEM_TQ, w), lambda b, t: (b * tq + t, 0)),
        out_shape=jax.ShapeDtypeStruct((batch * seq, MEM_WIDTH), BF16),
        compiler_params=_cparams(("arbitrary", "arbitrary"), 32),
        name="mem_attention",
    )(proj, kv, kv, proj)


def _out_proj_kernel(yd_ref, yc_ref, ym_ref, w_ref, zp_ref, mu_ref, rstd_ref, g_ref, b_ref,
                     z_ref, wb_ref):
    @pl.when(pl.program_id(1) == 0)
    def _():
        wb_ref[...] = w_ref[...].astype(BF16)

    c0, c1 = DIFF_WIDTH, DIFF_WIDTH + CONV_WIDTH
    acc = jnp.dot(yd_ref[...], wb_ref[0:c0, :], preferred_element_type=F32)
    acc = acc + jnp.dot(yc_ref[...], wb_ref[c0:c1, :], preferred_element_type=F32)
    acc = acc + jnp.dot(ym_ref[...], wb_ref[c1:D_MODEL, :], preferred_element_type=F32)
    tn = z_ref.shape[1]
    h = ((zp_ref[...] - _lane_tile(mu_ref[...], tn)) * _lane_tile(rstd_ref[...], tn) * g_ref[...]
         + b_ref[...])
    z_ref[...] = DEEPNORM_ALPHA * h + acc


def _out_proj(yd, yc, ym, w_out, layer, z_prev, mu, rstd, g_prev, b_prev, *, tm=512, tn=512):
    m = z_prev.shape[0]
    piece = pl.BlockSpec((tm, tn), lambda j, i: (i, j))
    stat = pl.BlockSpec((tm, LANES), lambda j, i: (i, 0))
    vec = pl.BlockSpec((1, tn), lambda j, i: (0, j))
    return pl.pallas_call(
        _out_proj_kernel,
        grid=(D_MODEL // tn, m // tm),
        in_specs=[pl.BlockSpec((tm, DIFF_WIDTH), lambda j, i: (i, 0)),
                  pl.BlockSpec((tm, CONV_WIDTH), lambda j, i: (i, 0)),
                  pl.BlockSpec((tm, MEM_WIDTH), lambda j, i: (i, 0)),
                  pl.BlockSpec((None, D_MODEL, tn), lambda j, i: (layer, 0, j)),
                  piece, stat, stat, vec, vec],
        out_specs=piece,
        out_shape=jax.ShapeDtypeStruct((m, D_MODEL), F32),
        scratch_shapes=[pltpu.VMEM((D_MODEL, tn), BF16)],
        compiler_params=_cparams(("arbitrary", "arbitrary"), 48),
        name="out_proj",
    )(yd, yc, ym, w_out, z_prev, mu, rstd, g_prev.reshape(1, D_MODEL), b_prev.reshape(1, D_MODEL))


def kernel(x, mem, ln_in_g, ln_in_b, w_in, lambda_q1, lambda_k1, lambda_q2, lambda_k2, subln_g,
           conv_w, conv_b, conv_ln_g, conv_ln_b, w_conv_pw, w_mem_kv, w_out, ln_post_g, ln_post_b):
    batch, seq, d = x.shape
    n_mem = mem.shape[1]
    assert d == D_MODEL and seq % ATTN_TQ == 0 and seq % CONV_TS == 0 and seq % MEM_TQ == 0
    z = x.reshape(batch * seq, d)
    ln_g, ln_b = ln_in_g, ln_in_b
    mem2 = mem.reshape(batch * n_mem, d)
    for layer in range(DEPTH):
        lam_init = 0.8 - 0.6 * math.exp(-0.3 * layer)
        h_bf16, mu, rstd = _layer_norm(z, ln_g, ln_b, final=False)
        proj = _matmul_ws(h_bf16, w_in, layer, tm=512, tn=1024,
                          n_scaled_tiles=DIFF_WIDTH // 1024, scale=Q_DIFF_SCALE, name="in_proj")
        kv = _matmul_ws(mem2, w_mem_kv, layer, tm=512, tn=512, vmem_mib=48, name="mem_kv")
        yd = _diff_attention(proj, lambda_q1, lambda_k1, lambda_q2, lambda_k2, subln_g, layer,
                             batch=batch, seq=seq, lam_init=lam_init)
        yc = _conv_branch(proj, conv_w, conv_b, conv_ln_g, conv_ln_b, w_conv_pw, layer, seq=seq)
        ym = _mem_attention(proj, kv, batch=batch, seq=seq, n_mem=n_mem)
        z = _out_proj(yd, yc, ym, w_out, layer, z, mu, rstd, ln_g, ln_b)
        ln_g, ln_b = ln_post_g[layer], ln_post_b[layer]
    return _layer_norm(z, ln_g, ln_b, final=True).reshape(batch, seq, d)
```

```python
import functools
import math

import jax
import jax.numpy as jnp
from jax import lax
from jax.experimental import pallas as pl
from jax.experimental.pallas import tpu as pltpu

D_MODEL = 4096
DEPTH = 2
CHUNK = 64
DIFF_WIDTH = D_MODEL // 2
CONV_WIDTH = D_MODEL // 4
MEM_WIDTH = D_MODEL // 4
DIFF_HEADS = 8
DIFF_HEAD_DIM = DIFF_WIDTH // (2 * DIFF_HEADS)
DIFF_V_DIM = 2 * DIFF_HEAD_DIM
MEM_HEADS = 4
MEM_HEAD_DIM = MEM_WIDTH // MEM_HEADS
CONV_K = 31
LN_EPS = 1e-5
RMS_EPS = 1e-5
DEEPNORM_ALPHA = (2.0 * DEPTH) ** 0.25

OFF_Q = 0
OFF_K = OFF_Q + DIFF_WIDTH
OFF_V = OFF_K + DIFF_WIDTH
OFF_GD = OFF_V + DIFF_WIDTH
OFF_GLU_A = OFF_GD + DIFF_WIDTH
OFF_GLU_G = OFF_GLU_A + CONV_WIDTH
OFF_GC = OFF_GLU_G + CONV_WIDTH
OFF_QM = OFF_GC + CONV_WIDTH
OFF_GM = OFF_QM + MEM_WIDTH
IN_COLS = OFF_GM + MEM_WIDTH

LOG2E = 1.4426950408889634
Q_DIFF_SCALE = DIFF_HEAD_DIM ** -0.5 * LOG2E

LANES = 128
SUBLANES = 8
CONV_HALO = 32
ATTN_TQ = 256
MIB = 1024 * 1024

F32 = jnp.float32
BF16 = jnp.bfloat16


def _cparams(sem, vmem_mib):
    return pltpu.CompilerParams(dimension_semantics=sem, vmem_limit_bytes=int(vmem_mib * MIB))


def _silu(x):
    return x * jax.nn.sigmoid(x)


def _lane_tile(v, width):
    return jnp.concatenate([v] * (width // LANES), axis=1)


def _row_stats(x):
    mu = jnp.mean(x, axis=-1, keepdims=True)
    xc = x - mu
    var = jnp.mean(xc * xc, axis=-1, keepdims=True)
    return mu, xc, lax.rsqrt(var + LN_EPS)


def _ln_rows(x, g, b):
    _, xc, rstd = _row_stats(x)
    return xc * rstd * g + b


def _ln_stats_kernel(x_ref, g_ref, b_ref, hb_ref, mu_ref, rstd_ref):
    mu, xc, rstd = _row_stats(x_ref[...])
    hb_ref[...] = (xc * rstd * g_ref[...] + b_ref[...]).astype(BF16)
    mu_ref[...] = jnp.broadcast_to(mu, mu_ref.shape)
    rstd_ref[...] = jnp.broadcast_to(rstd, rstd_ref.shape)


def _ln_final_kernel(x_ref, g_ref, b_ref, o_ref):
    o_ref[...] = _ln_rows(x_ref[...], g_ref[...], b_ref[...])


def _layer_norm(x, g, b, *, final, tm=512):
    m, d = x.shape
    row = pl.BlockSpec((tm, d), lambda i: (i, 0))
    vec = pl.BlockSpec((1, d), lambda i: (0, 0))
    stat = pl.BlockSpec((tm, LANES), lambda i: (i, 0))
    if final:
        out_specs, out_shape, body = row, jax.ShapeDtypeStruct((m, d), F32), _ln_final_kernel
    else:
        out_specs = [row, stat, stat]
        out_shape = [jax.ShapeDtypeStruct((m, d), BF16), jax.ShapeDtypeStruct((m, LANES), F32),
                     jax.ShapeDtypeStruct((m, LANES), F32)]
        body = _ln_stats_kernel
    return pl.pallas_call(
        body, grid=(m // tm,), in_specs=[row, vec, vec], out_specs=out_specs, out_shape=out_shape,
        compiler_params=_cparams(("arbitrary",), 48),
        name="layer_norm_final" if final else "layer_norm",
    )(x, g.reshape(1, d), b.reshape(1, d))


def _matmul_kernel(a_ref, w_ref, o_ref, *, n_scaled_tiles, scale):
    acc = jnp.dot(a_ref[...].astype(BF16), w_ref[...].astype(BF16), preferred_element_type=F32)
    if n_scaled_tiles:
        acc = acc * jnp.where(pl.program_id(0) < n_scaled_tiles, scale, 1.0).astype(F32)
    o_ref[...] = acc.astype(o_ref.dtype)


def _matmul_ws(a, w, layer, *, tm, tn, n_scaled_tiles=0, scale=1.0, vmem_mib, name):
    m, k = a.shape
    n = w.shape[-1]
    return pl.pallas_call(
        functools.partial(_matmul_kernel, n_scaled_tiles=n_scaled_tiles, scale=scale),
        grid=(n // tn, m // tm),
        in_specs=[pl.BlockSpec((tm, k), lambda j, i: (i, 0)),
                  pl.BlockSpec((None, k, tn), lambda j, i: (layer, 0, j))],
        out_specs=pl.BlockSpec((tm, tn), lambda j, i: (i, j)),
        out_shape=jax.ShapeDtypeStruct((m, n), BF16),
        compiler_params=_cparams(("arbitrary", "arbitrary"), vmem_mib),
        name=name,
    )(a, w)


_NT_DIMS = (((1,), (1,)), ((), ()))


def _diff_attn_kernel(q_ref, k_ref, v_ref, g_ref, lq1_ref, lk1_ref, lq2_ref, lk2_ref, sg_ref,
                      o_ref, s_ref, *, lam_init, seq):
    d = DIFF_HEAD_DIM
    lam = (jnp.exp(jnp.sum(lq1_ref[...] * lk1_ref[...], axis=-1, keepdims=True))
           - jnp.exp(jnp.sum(lq2_ref[...] * lk2_ref[...], axis=-1, keepdims=True))
           + lam_init)
    row_chunk = lax.broadcasted_iota(jnp.int32, (ATTN_TQ, ATTN_TQ), 0) // CHUNK
    col_chunk = lax.broadcasted_iota(jnp.int32, (ATTN_TQ, ATTN_TQ), 1) // CHUNK
    mask = col_chunk <= row_chunk
    out_gain = sg_ref[...] * (1.0 - lam_init)
    row_max = {}

    def scores(t):
        lo = t * ATTN_TQ
        for m in range(2):
            cols = slice(m * d, (m + 1) * d)
            q = q_ref[lo:lo + ATTN_TQ, cols]
            s_d = lax.dot_general(q, k_ref[lo:lo + ATTN_TQ, cols], _NT_DIMS, preferred_element_type=F32)
            s_d = jnp.where(mask, s_d, -jnp.inf)
            s_ref[t % 2, m, :, lo:lo + ATTN_TQ] = s_d
            mx = jnp.max(s_d, axis=-1, keepdims=True)
            if lo:
                s_o = lax.dot_general(q, k_ref[0:lo, cols], _NT_DIMS, preferred_element_type=F32)
                s_ref[t % 2, m, :, 0:lo] = s_o
                mx = jnp.maximum(mx, jnp.max(s_o, axis=-1, keepdims=True))
            row_max[(t, m)] = mx

    def outputs(t):
        lo = t * ATTN_TQ
        kv = lo + ATTN_TQ
        o = []
        for m in range(2):
            p = jnp.exp2(s_ref[t % 2, m, :, 0:kv] - row_max[(t, m)])
            l = jnp.sum(p, axis=-1, keepdims=True)
            pv = jnp.dot(p.astype(BF16), v_ref[0:kv, :], preferred_element_type=F32)
            o.append(pv * ((lam if m else 1.0) / l))
        o = o[0] - o[1]
        o = o * lax.rsqrt(jnp.mean(o * o, axis=-1, keepdims=True) + RMS_EPS) * out_gain
        o_ref[lo:lo + ATTN_TQ, :] = (o * _silu(g_ref[lo:lo + ATTN_TQ, :].astype(F32))).astype(o_ref.dtype)

    n_tiles = seq // ATTN_TQ
    for t in range(n_tiles + 1):
        if t < n_tiles:
            scores(t)
        if t >= 1:
            outputs(t - 1)


def _diff_attention(proj, lq1, lk1, lq2, lk2, subln_g, layer, *, batch, seq, lam_init):
    w = DIFF_V_DIM
    head_block = lambda off: pl.BlockSpec((seq, w), lambda b, h: (b, off // w + h))
    lam_spec = pl.BlockSpec((None, 1, DIFF_HEAD_DIM), lambda b, h: (layer, 0, 0))
    return pl.pallas_call(
        functools.partial(_diff_attn_kernel, lam_init=lam_init, seq=seq),
        grid=(batch, DIFF_HEADS),
        in_specs=[head_block(OFF_Q), head_block(OFF_K), head_block(OFF_V), head_block(OFF_GD),
                  lam_spec, lam_spec, lam_spec, lam_spec,
                  pl.BlockSpec((None, 1, DIFF_V_DIM), lambda b, h: (layer, 0, 0))],
        out_specs=pl.BlockSpec((seq, w), lambda b, h: (b, h)),
        out_shape=jax.ShapeDtypeStruct((batch * seq, DIFF_WIDTH), BF16),
        scratch_shapes=[pltpu.VMEM((2, 2, ATTN_TQ, seq), F32)],
        compiler_params=_cparams(("arbitrary", "arbitrary"), 48),
        name="diff_attention",
    )(proj, proj, proj, proj,
      lq1.reshape(DEPTH, 1, -1), lk1.reshape(DEPTH, 1, -1), lq2.reshape(DEPTH, 1, -1),
      lk2.reshape(DEPTH, 1, -1), subln_g.reshape(DEPTH, 1, -1))


CONV_TS = 256
CONV_RC = 32
CONV_PAD = CONV_HALO - (CONV_K - 1)


def _conv_kernel(a_ref, gt_ref, ah_ref, gh_ref, gc_ref, cw_ref, cb_ref, lg_ref, lb_ref, wpw_ref,
                 o_ref, hbuf_ref, cout_ref, wtap_ref, wpw_bf_ref, *, tiles_per_seq):
    i = pl.program_id(0)

    @pl.when(i == 0)
    def _():
        wpw_bf_ref[...] = wpw_ref[...].astype(BF16)
        for j in range(CONV_K):
            wtap_ref[j * SUBLANES:(j + 1) * SUBLANES, :] = jnp.broadcast_to(
                cw_ref[j:j + 1, :], (SUBLANES, CONV_WIDTH))

    halo = ah_ref[...].astype(F32) * jax.nn.sigmoid(gh_ref[...].astype(F32))
    keep = (i % tiles_per_seq != 0).astype(F32)
    hbuf_ref[0:CONV_HALO, :] = halo * keep
    hbuf_ref[CONV_HALO:CONV_HALO + CONV_TS, :] = (
        a_ref[...].astype(F32) * jax.nn.sigmoid(gt_ref[...].astype(F32)))

    n_sub = CONV_RC // SUBLANES
    for r in range(CONV_TS // CONV_RC):
        acc = [jnp.zeros((SUBLANES, CONV_WIDTH), F32) for _ in range(n_sub)]
        for j in range(CONV_K):
            wj = wtap_ref[j * SUBLANES:(j + 1) * SUBLANES, :]
            for s in range(n_sub):
                start = CONV_PAD + r * CONV_RC + s * SUBLANES + j
                acc[s] = acc[s] + wj * hbuf_ref[start:start + SUBLANES, :]
        for s in range(n_sub):
            row = r * CONV_RC + s * SUBLANES
            cout_ref[row:row + SUBLANES, :] = acc[s]

    c = cout_ref[...] + cb_ref[...]
    c = _silu(_ln_rows(c, lg_ref[...], lb_ref[...]))
    y = jnp.dot(c.astype(BF16), wpw_bf_ref[...], preferred_element_type=F32)
    o_ref[...] = (y * _silu(gc_ref[...].astype(F32))).astype(o_ref.dtype)


def _conv_branch(proj, conv_w, conv_b, ln_g, ln_b, w_pw, layer, *, seq):
    m = proj.shape[0]
    cw = CONV_WIDTH
    halo_per_tile = CONV_TS // CONV_HALO
    main = lambda off: pl.BlockSpec((CONV_TS, cw), lambda i: (i, off // cw))
    halo = lambda off: pl.BlockSpec(
        (CONV_HALO, cw), lambda i: (jnp.maximum(i * halo_per_tile - 1, 0), off // cw))
    vec = pl.BlockSpec((None, 1, cw), lambda i: (layer, 0, 0))
    return pl.pallas_call(
        functools.partial(_conv_kernel, tiles_per_seq=seq // CONV_TS),
        grid=(m // CONV_TS,),
        in_specs=[main(OFF_GLU_A), main(OFF_GLU_G), halo(OFF_GLU_A), halo(OFF_GLU_G), main(OFF_GC),
                  pl.BlockSpec((None, CONV_K, cw), lambda i: (layer, 0, 0)),
                  vec, vec, vec,
                  pl.BlockSpec((None, cw, cw), lambda i: (layer, 0, 0))],
        out_specs=pl.BlockSpec((CONV_TS, cw), lambda i: (i, 0)),
        out_shape=jax.ShapeDtypeStruct((m, cw), BF16),
        scratch_shapes=[pltpu.VMEM((CONV_HALO + CONV_TS, cw), F32),
                        pltpu.VMEM((CONV_TS, cw), F32),
                        pltpu.VMEM((CONV_K * SUBLANES, cw), F32),
                        pltpu.VMEM((cw, cw), BF16)],
        compiler_params=_cparams(("arbitrary",), 40),
        name="conv_branch",
    )(proj, proj, proj, proj, proj, conv_w, conv_b.reshape(DEPTH, 1, cw),
      ln_g.reshape(DEPTH, 1, cw), ln_b.reshape(DEPTH, 1, cw), w_pw)


MEM_TQ = 512


def _mem_attn_kernel(q_ref, k_ref, v_ref, g_ref, o_ref):
    for h in range(MEM_HEADS):
        cols = slice(h * MEM_HEAD_DIM, (h + 1) * MEM_HEAD_DIM)
        s = lax.dot_general(q_ref[:, cols], k_ref[:, cols], _NT_DIMS, preferred_element_type=F32)
        s = s * (MEM_HEAD_DIM ** -0.5)
        p = jnp.exp(s - jnp.max(s, axis=-1, keepdims=True))
        l = jnp.sum(p, axis=-1, keepdims=True)
        o = jnp.dot(p.astype(BF16), v_ref[:, cols], preferred_element_type=F32) * (1.0 / l)
        o_ref[:, cols] = (o * _silu(g_ref[:, cols].astype(F32))).astype(o_ref.dtype)


def _mem_attention(proj, kv, *, batch, seq, n_mem):
    w = MEM_WIDTH
    tq = seq // MEM_TQ
    return pl.pallas_call(
        _mem_attn_kernel,
        grid=(batch, tq),
        in_specs=[pl.BlockSpec((MEM_TQ, w), lambda b, t: (b * tq + t, OFF_QM // w)),
                  pl.BlockSpec((n_mem, w), lambda b, t: (b, 0)),
                  pl.BlockSpec((n_mem, w), lambda b, t: (b, 1)),
                  pl.BlockSpec((MEM_TQ, w), lambda b, t: (b * tq + t, OFF_GM // w))],
        out_specs=pl.BlockSpec((MEM_TQ, w), lambda b, t: (b * tq + t, 0)),
        out_shape=jax.ShapeDtypeStruct((batch * seq, MEM_WIDTH), BF16),
        compiler_params=_cparams(("arbitrary", "arbitrary"), 32),
        name="mem_attention",
    )(proj, kv, kv, proj)


def _out_proj_kernel(yd_ref, yc_ref, ym_ref, w_ref, zp_ref, mu_ref, rstd_ref, g_ref, b_ref, z_ref):
    c0, c1 = DIFF_WIDTH, DIFF_WIDTH + CONV_WIDTH
    acc = jnp.dot(yd_ref[...], w_ref[0:c0, :].astype(BF16), preferred_element_type=F32)
    acc = acc + jnp.dot(yc_ref[...], w_ref[c0:c1, :].astype(BF16), preferred_element_type=F32)
    acc = acc + jnp.dot(ym_ref[...], w_ref[c1:D_MODEL, :].astype(BF16), preferred_element_type=F32)
    tn = z_ref.shape[1]
    h = ((zp_ref[...] - _lane_tile(mu_ref[...], tn)) * _lane_tile(rstd_ref[...], tn) * g_ref[...]
         + b_ref[...])
    z_ref[...] = DEEPNORM_ALPHA * h + acc


def _out_proj(yd, yc, ym, w_out, layer, z_prev, mu, rstd, g_prev, b_prev, *, tm=1024, tn=512):
    m = z_prev.shape[0]
    piece = pl.BlockSpec((tm, tn), lambda j, i: (i, j))
    stat = pl.BlockSpec((tm, LANES), lambda j, i: (i, 0))
    vec = pl.BlockSpec((1, tn), lambda j, i: (0, j))
    return pl.pallas_call(
        _out_proj_kernel,
        grid=(D_MODEL // tn, m // tm),
        in_specs=[pl.BlockSpec((tm, DIFF_WIDTH), lambda j, i: (i, 0)),
                  pl.BlockSpec((tm, CONV_WIDTH), lambda j, i: (i, 0)),
                  pl.BlockSpec((tm, MEM_WIDTH), lambda j, i: (i, 0)),
                  pl.BlockSpec((None, D_MODEL, tn), lambda j, i: (layer, 0, j)),
                  piece, stat, stat, vec, vec],
        out_specs=piece,
        out_shape=jax.ShapeDtypeStruct((m, D_MODEL), F32),
        compiler_params=_cparams(("arbitrary", "arbitrary"), 52),
        name="out_proj",
    )(yd, yc, ym, w_out, z_prev, mu, rstd, g_prev.reshape(1, D_MODEL), b_prev.reshape(1, D_MODEL))


def kernel(x, mem, ln_in_g, ln_in_b, w_in, lambda_q1, lambda_k1, lambda_q2, lambda_k2, subln_g,
           conv_w, conv_b, conv_ln_g, conv_ln_b, w_conv_pw, w_mem_kv, w_out, ln_post_g, ln_post_b):
    batch, seq, d = x.shape
    n_mem = mem.shape[1]
    assert d == D_MODEL and seq % ATTN_TQ == 0 and seq % CONV_TS == 0 and seq % MEM_TQ == 0
    z = x.reshape(batch * seq, d)
    ln_g, ln_b = ln_in_g, ln_in_b
    mem2 = mem.reshape(batch * n_mem, d)
    for layer in range(DEPTH):
        lam_init = 0.8 - 0.6 * math.exp(-0.3 * layer)
        h_bf16, mu, rstd = _layer_norm(z, ln_g, ln_b, final=False)
        proj = _matmul_ws(h_bf16, w_in, layer, tm=2048, tn=512, n_scaled_tiles=DIFF_WIDTH // 512,
                          scale=Q_DIFF_SCALE, vmem_mib=58, name="in_proj")
        kv = _matmul_ws(mem2, w_mem_kv, layer, tm=512, tn=512, vmem_mib=48, name="mem_kv")
        yd = _diff_attention(proj, lambda_q1, lambda_k1, lambda_q2, lambda_k2, subln_g, layer,
                             batch=batch, seq=seq, lam_init=lam_init)
        yc = _conv_branch(proj, conv_w, conv_b, conv_ln_g, conv_ln_b, w_conv_pw, layer, seq=seq)
        ym = _mem_attention(proj, kv, batch=batch, seq=seq, n_mem=n_mem)
        z = _out_proj(yd, yc, ym, w_out, layer, z, mu, rstd, ln_g, ln_b)
        ln_g, ln_b = ln_post_g[layer], ln_post_b[layer]
    return _layer_norm(z, ln_g, ln_b, final=True).reshape(batch, seq, d)
```

```python
import functools
import math

import jax
import jax.numpy as jnp
from jax import lax
from jax.experimental import pallas as pl
from jax.experimental.pallas import tpu as pltpu

D_MODEL = 4096
DEPTH = 2
CHUNK = 64
DIFF_WIDTH = D_MODEL // 2
CONV_WIDTH = D_MODEL // 4
MEM_WIDTH = D_MODEL // 4
DIFF_HEADS = 8
DIFF_HEAD_DIM = DIFF_WIDTH // (2 * DIFF_HEADS)
DIFF_V_DIM = 2 * DIFF_HEAD_DIM
MEM_HEADS = 4
MEM_HEAD_DIM = MEM_WIDTH // MEM_HEADS
CONV_K = 31
LN_EPS = 1e-5
RMS_EPS = 1e-5
DEEPNORM_ALPHA = (2.0 * DEPTH) ** 0.25

OFF_Q = 0
OFF_K = OFF_Q + DIFF_WIDTH
OFF_V = OFF_K + DIFF_WIDTH
OFF_GD = OFF_V + DIFF_WIDTH
OFF_GLU_A = OFF_GD + DIFF_WIDTH
OFF_GLU_G = OFF_GLU_A + CONV_WIDTH
OFF_GC = OFF_GLU_G + CONV_WIDTH
OFF_QM = OFF_GC + CONV_WIDTH
OFF_GM = OFF_QM + MEM_WIDTH
IN_COLS = OFF_GM + MEM_WIDTH

LOG2E = 1.4426950408889634
Q_DIFF_SCALE = DIFF_HEAD_DIM ** -0.5 * LOG2E

LANES = 128
SUBLANES = 8
CONV_HALO = 32
ATTN_TQ = 256
MIB = 1024 * 1024

F32 = jnp.float32
BF16 = jnp.bfloat16


def _cparams(sem, vmem_mib):
    return pltpu.CompilerParams(dimension_semantics=sem, vmem_limit_bytes=int(vmem_mib * MIB))


def _silu(x):
    return x * jax.nn.sigmoid(x)


def _lane_tile(v, width):
    return jnp.concatenate([v] * (width // LANES), axis=1)


def _row_stats(x):
    mu = jnp.mean(x, axis=-1, keepdims=True)
    xc = x - mu
    var = jnp.mean(xc * xc, axis=-1, keepdims=True)
    return mu, xc, lax.rsqrt(var + LN_EPS)


def _ln_rows(x, g, b):
    _, xc, rstd = _row_stats(x)
    return xc * rstd * g + b


def _ln_stats_kernel(x_ref, g_ref, b_ref, hb_ref, mu_ref, rstd_ref):
    mu, xc, rstd = _row_stats(x_ref[...])
    hb_ref[...] = (xc * rstd * g_ref[...] + b_ref[...]).astype(BF16)
    mu_ref[...] = jnp.broadcast_to(mu, mu_ref.shape)
    rstd_ref[...] = jnp.broadcast_to(rstd, rstd_ref.shape)


def _ln_final_kernel(x_ref, g_ref, b_ref, o_ref):
    o_ref[...] = _ln_rows(x_ref[...], g_ref[...], b_ref[...])


def _layer_norm(x, g, b, *, final, tm=512):
    m, d = x.shape
    row = pl.BlockSpec((tm, d), lambda i: (i, 0))
    vec = pl.BlockSpec((1, d), lambda i: (0, 0))
    stat = pl.BlockSpec((tm, LANES), lambda i: (i, 0))
    if final:
        out_specs, out_shape, body = row, jax.ShapeDtypeStruct((m, d), F32), _ln_final_kernel
    else:
        out_specs = [row, stat, stat]
        out_shape = [jax.ShapeDtypeStruct((m, d), BF16), jax.ShapeDtypeStruct((m, LANES), F32),
                     jax.ShapeDtypeStruct((m, LANES), F32)]
        body = _ln_stats_kernel
    return pl.pallas_call(
        body, grid=(m // tm,), in_specs=[row, vec, vec], out_specs=out_specs, out_shape=out_shape,
        compiler_params=_cparams(("arbitrary",), 48),
        name="layer_norm_final" if final else "layer_norm",
    )(x, g.reshape(1, d), b.reshape(1, d))


IN_PROJ_SCALED = ((OFF_Q, OFF_K, Q_DIFF_SCALE), (OFF_QM, OFF_GM, MEM_HEAD_DIM ** -0.5))


def _matmul_kernel(a_ref, w_ref, o_ref, *, col_scales):
    acc = jnp.dot(a_ref[...].astype(BF16), w_ref[...].astype(BF16), preferred_element_type=F32)
    if col_scales:
        col = pl.program_id(0) * o_ref.shape[1]
        scale = jnp.float32(1.0)
        for lo, hi, s in col_scales:
            scale = jnp.where((col >= lo) & (col < hi), jnp.float32(s), scale)
        acc = acc * scale
    o_ref[...] = acc.astype(o_ref.dtype)


def _matmul_ws(a, w, layer, *, tm, tn, col_scales=(), vmem_mib, name):
    m, k = a.shape
    n = w.shape[-1]
    assert all(lo % tn == 0 and hi % tn == 0 for lo, hi, _ in col_scales)
    return pl.pallas_call(
        functools.partial(_matmul_kernel, col_scales=col_scales),
        grid=(n // tn, m // tm),
        in_specs=[pl.BlockSpec((tm, k), lambda j, i: (i, 0)),
                  pl.BlockSpec((None, k, tn), lambda j, i: (layer, 0, j))],
        out_specs=pl.BlockSpec((tm, tn), lambda j, i: (i, j)),
        out_shape=jax.ShapeDtypeStruct((m, n), BF16),
        compiler_params=_cparams(("arbitrary", "arbitrary"), vmem_mib),
        name=name,
    )(a, w)


_NT_DIMS = (((1,), (1,)), ((), ()))


def _diff_attn_kernel(q_ref, k_ref, v_ref, g_ref, lq1_ref, lk1_ref, lq2_ref, lk2_ref, sg_ref,
                      o_ref, s_ref, *, lam_init, seq):
    d = DIFF_HEAD_DIM
    lam = (jnp.exp(jnp.sum(lq1_ref[...] * lk1_ref[...], axis=-1, keepdims=True))
           - jnp.exp(jnp.sum(lq2_ref[...] * lk2_ref[...], axis=-1, keepdims=True))
           + lam_init)
    row_chunk = lax.broadcasted_iota(jnp.int32, (ATTN_TQ, ATTN_TQ), 0) // CHUNK
    col_chunk = lax.broadcasted_iota(jnp.int32, (ATTN_TQ, ATTN_TQ), 1) // CHUNK
    mask = col_chunk <= row_chunk
    out_gain = sg_ref[...] * (1.0 - lam_init)
    row_max = {}

    def scores(t):
        lo = t * ATTN_TQ
        for m in range(2):
            cols = slice(m * d, (m + 1) * d)
            q = q_ref[lo:lo + ATTN_TQ, cols]
            s_d = lax.dot_general(q, k_ref[lo:lo + ATTN_TQ, cols], _NT_DIMS, preferred_element_type=F32)
            s_d = jnp.where(mask, s_d, -jnp.inf)
            s_ref[t % 2, m, :, lo:lo + ATTN_TQ] = s_d
            mx = jnp.max(s_d, axis=-1, keepdims=True)
            if lo:
                s_o = lax.dot_general(q, k_ref[0:lo, cols], _NT_DIMS, preferred_element_type=F32)
                s_ref[t % 2, m, :, 0:lo] = s_o
                mx = jnp.maximum(mx, jnp.max(s_o, axis=-1, keepdims=True))
            row_max[(t, m)] = mx

    def outputs(t):
        lo = t * ATTN_TQ
        kv = lo + ATTN_TQ
        p = []
        for m in range(2):
            e = jnp.exp2(s_ref[t % 2, m, :, 0:kv] - row_max[(t, m)])
            l = jnp.sum(e, axis=-1, keepdims=True)
            p.append(e * ((lam if m else 1.0) / l))
        o = jnp.dot((p[0] - p[1]).astype(BF16), v_ref[0:kv, :], preferred_element_type=F32)
        o = o * lax.rsqrt(jnp.mean(o * o, axis=-1, keepdims=True) + RMS_EPS) * out_gain
        o_ref[lo:lo + ATTN_TQ, :] = (o * _silu(g_ref[lo:lo + ATTN_TQ, :].astype(F32))).astype(o_ref.dtype)

    n_tiles = seq // ATTN_TQ
    for t in range(n_tiles, -1, -1):
        if t >= 1:
            scores(t - 1)
        if t < n_tiles:
            outputs(t)


def _diff_attention(proj, lq1, lk1, lq2, lk2, subln_g, layer, *, batch, seq, lam_init):
    w = DIFF_V_DIM
    head_block = lambda off: pl.BlockSpec((seq, w), lambda b, h: (b, off // w + h))
    lam_spec = pl.BlockSpec((None, 1, DIFF_HEAD_DIM), lambda b, h: (layer, 0, 0))
    return pl.pallas_call(
        functools.partial(_diff_attn_kernel, lam_init=lam_init, seq=seq),
        grid=(batch, DIFF_HEADS),
        in_specs=[head_block(OFF_Q), head_block(OFF_K), head_block(OFF_V), head_block(OFF_GD),
                  lam_spec, lam_spec, lam_spec, lam_spec,
                  pl.BlockSpec((None, 1, DIFF_V_DIM), lambda b, h: (layer, 0, 0))],
        out_specs=pl.BlockSpec((seq, w), lambda b, h: (b, h)),
        out_shape=jax.ShapeDtypeStruct((batch * seq, DIFF_WIDTH), BF16),
        scratch_shapes=[pltpu.VMEM((2, 2, ATTN_TQ, seq), F32)],
        compiler_params=_cparams(("arbitrary", "arbitrary"), 48),
        name="diff_attention",
    )(proj, proj, proj, proj,
      lq1.reshape(DEPTH, 1, -1), lk1.reshape(DEPTH, 1, -1), lq2.reshape(DEPTH, 1, -1),
      lk2.reshape(DEPTH, 1, -1), subln_g.reshape(DEPTH, 1, -1))


CONV_TS = 256
CONV_RC = 32
CONV_PAD = CONV_HALO - (CONV_K - 1)


def _conv_kernel(a_ref, gt_ref, ah_ref, gh_ref, gc_ref, cw_ref, cb_ref, lg_ref, lb_ref, wpw_ref,
                 o_ref, hbuf_ref, shift_ref, cout_ref, wtap_ref, wpw_bf_ref, *, tiles_per_seq):
    i = pl.program_id(0)

    @pl.when(i == 0)
    def _():
        wpw_bf_ref[...] = wpw_ref[...].astype(BF16)
        for j in range(CONV_K):
            wtap_ref[j * SUBLANES:(j + 1) * SUBLANES, :] = jnp.broadcast_to(
                cw_ref[j:j + 1, :], (SUBLANES, CONV_WIDTH))

    halo = ah_ref[...].astype(F32) * jax.nn.sigmoid(gh_ref[...].astype(F32))
    keep = (i % tiles_per_seq != 0).astype(F32)
    hbuf_ref[0:CONV_HALO, :] = halo * keep
    hbuf_ref[CONV_HALO:CONV_HALO + CONV_TS, :] = (
        a_ref[...].astype(F32) * jax.nn.sigmoid(gt_ref[...].astype(F32)))

    n_shifted = CONV_HALO + CONV_TS - SUBLANES
    for k in range(1, SUBLANES):
        shift_ref[k - 1, 0:n_shifted, :] = hbuf_ref[k:k + n_shifted, :]
    n_sub = CONV_RC // SUBLANES
    for r in range(CONV_TS // CONV_RC):
        acc = [jnp.zeros((SUBLANES, CONV_WIDTH), F32) for _ in range(n_sub)]
        for j in range(CONV_K):
            k, base = (CONV_PAD + j) % SUBLANES, (CONV_PAD + j) // SUBLANES * SUBLANES
            wj = wtap_ref[j * SUBLANES:(j + 1) * SUBLANES, :]
            for s in range(n_sub):
                row = base + r * CONV_RC + s * SUBLANES
                win = shift_ref[k - 1, row:row + SUBLANES, :] if k else hbuf_ref[row:row + SUBLANES, :]
                acc[s] = acc[s] + wj * win
        for s in range(n_sub):
            row = r * CONV_RC + s * SUBLANES
            cout_ref[row:row + SUBLANES, :] = acc[s]

    c = cout_ref[...] + cb_ref[...]
    c = _silu(_ln_rows(c, lg_ref[...], lb_ref[...]))
    y = jnp.dot(c.astype(BF16), wpw_bf_ref[...], preferred_element_type=F32)
    o_ref[...] = (y * _silu(gc_ref[...].astype(F32))).astype(o_ref.dtype)


def _conv_branch(proj, conv_w, conv_b, ln_g, ln_b, w_pw, layer, *, seq):
    m = proj.shape[0]
    cw = CONV_WIDTH
    halo_per_tile = CONV_TS // CONV_HALO
    main = lambda off: pl.BlockSpec((CONV_TS, cw), lambda i: (i, off // cw))
    halo = lambda off: pl.BlockSpec(
        (CONV_HALO, cw), lambda i: (jnp.maximum(i * halo_per_tile - 1, 0), off // cw))
    vec = pl.BlockSpec((None, 1, cw), lambda i: (layer, 0, 0))
    return pl.pallas_call(
        functools.partial(_conv_kernel, tiles_per_seq=seq // CONV_TS),
        grid=(m // CONV_TS,),
        in_specs=[main(OFF_GLU_A), main(OFF_GLU_G), halo(OFF_GLU_A), halo(OFF_GLU_G), main(OFF_GC),
                  pl.BlockSpec((None, CONV_K, cw), lambda i: (layer, 0, 0)),
                  vec, vec, vec,
                  pl.BlockSpec((None, cw, cw), lambda i: (layer, 0, 0))],
        out_specs=pl.BlockSpec((CONV_TS, cw), lambda i: (i, 0)),
        out_shape=jax.ShapeDtypeStruct((m, cw), BF16),
        scratch_shapes=[pltpu.VMEM((CONV_HALO + CONV_TS, cw), F32),
                        pltpu.VMEM((SUBLANES - 1, CONV_HALO + CONV_TS, cw), F32),
                        pltpu.VMEM((CONV_TS, cw), F32),
                        pltpu.VMEM((CONV_K * SUBLANES, cw), F32),
                        pltpu.VMEM((cw, cw), BF16)],
        compiler_params=_cparams(("arbitrary",), 48),
        name="conv_branch",
    )(proj, proj, proj, proj, proj, conv_w, conv_b.reshape(DEPTH, 1, cw),
      ln_g.reshape(DEPTH, 1, cw), ln_b.reshape(DEPTH, 1, cw), w_pw)


MEM_TQ = 512


def _mem_attn_kernel(q_ref, k_ref, v_ref, g_ref, o_ref):
    for h in range(MEM_HEADS):
        cols = slice(h * MEM_HEAD_DIM, (h + 1) * MEM_HEAD_DIM)
        s = lax.dot_general(q_ref[:, cols], k_ref[:, cols], _NT_DIMS, preferred_element_type=F32)
        p = jnp.exp(s - jnp.max(s, axis=-1, keepdims=True))
        l = jnp.sum(p, axis=-1, keepdims=True)
        o = jnp.dot(p.astype(BF16), v_ref[:, cols], preferred_element_type=F32) * (1.0 / l)
        o_ref[:, cols] = (o * _silu(g_ref[:, cols].astype(F32))).astype(o_ref.dtype)


def _mem_attention(proj, kv, *, batch, seq, n_mem):
    w = MEM_WIDTH
    tq = seq // MEM_TQ
    return pl.pallas_call(
        _mem_attn_kernel,
        grid=(batch, tq),
        in_specs=[pl.BlockSpec((MEM_TQ, w), lambda b, t: (b * tq + t, OFF_QM // w)),
                  pl.BlockSpec((n_mem, w), lambda b, t: (b, 0)),
                  pl.BlockSpec((n_mem, w), lambda b, t: (b, 1)),
                  pl.BlockSpec((MEM_TQ, w), lambda b, t: (b * tq + t, OFF_GM // w))],
        out_specs=pl.BlockSpec((MEM_TQ, w), lambda b, t: (b * tq + t, 0)),
        out_shape=jax.ShapeDtypeStruct((batch * seq, MEM_WIDTH), BF16),
        compiler_params=_cparams(("arbitrary", "arbitrary"), 32),
        name="mem_attention",
    )(proj, kv, kv, proj)


def _out_proj_kernel(yd_ref, yc_ref, ym_ref, w_ref, zp_ref, mu_ref, rstd_ref, g_ref, b_ref, z_ref):
    c0, c1 = DIFF_WIDTH, DIFF_WIDTH + CONV_WIDTH
    acc = jnp.dot(yd_ref[...], w_ref[0:c0, :].astype(BF16), preferred_element_type=F32)
    acc = acc + jnp.dot(yc_ref[...], w_ref[c0:c1, :].astype(BF16), preferred_element_type=F32)
    acc = acc + jnp.dot(ym_ref[...], w_ref[c1:D_MODEL, :].astype(BF16), preferred_element_type=F32)
    tn = z_ref.shape[1]
    h = ((zp_ref[...] - _lane_tile(mu_ref[...], tn)) * _lane_tile(rstd_ref[...], tn) * g_ref[...]
         + b_ref[...])
    z_ref[...] = DEEPNORM_ALPHA * h + acc


def _out_proj(yd, yc, ym, w_out, layer, z_prev, mu, rstd, g_prev, b_prev, *, tm=512, tn=1024):
    m = z_prev.shape[0]
    piece = pl.BlockSpec((tm, tn), lambda j, i: (i, j))
    stat = pl.BlockSpec((tm, LANES), lambda j, i: (i, 0))
    vec = pl.BlockSpec((1, tn), lambda j, i: (0, j))
    return pl.pallas_call(
        _out_proj_kernel,
        grid=(D_MODEL // tn, m // tm),
        in_specs=[pl.BlockSpec((tm, DIFF_WIDTH), lambda j, i: (i, 0)),
                  pl.BlockSpec((tm, CONV_WIDTH), lambda j, i: (i, 0)),
                  pl.BlockSpec((tm, MEM_WIDTH), lambda j, i: (i, 0)),
                  pl.BlockSpec((None, D_MODEL, tn), lambda j, i: (layer, 0, j)),
                  piece, stat, stat, vec, vec],
        out_specs=piece,
        out_shape=jax.ShapeDtypeStruct((m, D_MODEL), F32),
        compiler_params=_cparams(("arbitrary", "arbitrary"), 56),
        name="out_proj",
    )(yd, yc, ym, w_out, z_prev, mu, rstd, g_prev.reshape(1, D_MODEL), b_prev.reshape(1, D_MODEL))


def kernel(x, mem, ln_in_g, ln_in_b, w_in, lambda_q1, lambda_k1, lambda_q2, lambda_k2, subln_g,
           conv_w, conv_b, conv_ln_g, conv_ln_b, w_conv_pw, w_mem_kv, w_out, ln_post_g, ln_post_b):
    batch, seq, d = x.shape
    n_mem = mem.shape[1]
    assert d == D_MODEL and seq % ATTN_TQ == 0 and seq % CONV_TS == 0 and seq % MEM_TQ == 0
    z = x.reshape(batch * seq, d)
    ln_g, ln_b = ln_in_g, ln_in_b
    mem2 = mem.reshape(batch * n_mem, d)
    for layer in range(DEPTH):
        lam_init = 0.8 - 0.6 * math.exp(-0.3 * layer)
        h_bf16, mu, rstd = _layer_norm(z, ln_g, ln_b, final=False)
        proj = _matmul_ws(h_bf16, w_in, layer, tm=2048, tn=512, col_scales=IN_PROJ_SCALED, vmem_mib=58,
                          name="in_proj")
        kv = _matmul_ws(mem2, w_mem_kv, layer, tm=512, tn=512, vmem_mib=48, name="mem_kv")
        yd = _diff_attention(proj, lambda_q1, lambda_k1, lambda_q2, lambda_k2, subln_g, layer,
                             batch=batch, seq=seq, lam_init=lam_init)
        yc = _conv_branch(proj, conv_w, conv_b, conv_ln_g, conv_ln_b, w_conv_pw, layer, seq=seq)
        ym = _mem_attention(proj, kv, batch=batch, seq=seq, n_mem=n_mem)
        z = _out_proj(yd, yc, ym, w_out, layer, z, mu, rstd, ln_g, ln_b)
        ln_g, ln_b = ln_post_g[layer], ln_post_b[layer]
    return _layer_norm(z, ln_g, ln_b, final=True).reshape(batch, seq, d)
```

```python
import functools
import math

import jax
import jax.numpy as jnp
from jax import lax
from jax.experimental import pallas as pl
from jax.experimental.pallas import tpu as pltpu

D_MODEL = 4096
DEPTH = 2
CHUNK = 64
DIFF_WIDTH = D_MODEL // 2
CONV_WIDTH = D_MODEL // 4
MEM_WIDTH = D_MODEL // 4
DIFF_HEADS = 8
DIFF_HEAD_DIM = DIFF_WIDTH // (2 * DIFF_HEADS)
DIFF_V_DIM = 2 * DIFF_HEAD_DIM
MEM_HEADS = 4
MEM_HEAD_DIM = MEM_WIDTH // MEM_HEADS
CONV_K = 31
LN_EPS = 1e-5
RMS_EPS = 1e-5
DEEPNORM_ALPHA = (2.0 * DEPTH) ** 0.25

OFF_Q = 0
OFF_K = OFF_Q + DIFF_WIDTH
OFF_V = OFF_K + DIFF_WIDTH
OFF_GD = OFF_V + DIFF_WIDTH
OFF_GLU_A = OFF_GD + DIFF_WIDTH
OFF_GLU_G = OFF_GLU_A + CONV_WIDTH
OFF_GC = OFF_GLU_G + CONV_WIDTH
OFF_QM = OFF_GC + CONV_WIDTH
OFF_GM = OFF_QM + MEM_WIDTH
IN_COLS = OFF_GM + MEM_WIDTH

LOG2E = 1.4426950408889634
Q_DIFF_SCALE = DIFF_HEAD_DIM ** -0.5 * LOG2E

LANES = 128
SUBLANES = 8
CONV_HALO = 32
ATTN_TQ = 256
MIB = 1024 * 1024

F32 = jnp.float32
BF16 = jnp.bfloat16


def _cparams(sem, vmem_mib):
    return pltpu.CompilerParams(dimension_semantics=sem, vmem_limit_bytes=int(vmem_mib * MIB))


def _silu(x):
    return x * jax.nn.sigmoid(x)


def _lane_tile(v, width):
    return jnp.concatenate([v] * (width // LANES), axis=1)


def _row_stats(x):
    mu = jnp.mean(x, axis=-1, keepdims=True)
    xc = x - mu
    var = jnp.mean(xc * xc, axis=-1, keepdims=True)
    return mu, xc, lax.rsqrt(var + LN_EPS)


def _ln_rows(x, g, b):
    _, xc, rstd = _row_stats(x)
    return xc * rstd * g + b


def _ln_stats_kernel(x_ref, g_ref, b_ref, hb_ref, mu_ref, rstd_ref):
    g, b = g_ref[...], b_ref[...]
    for r in range(0, x_ref.shape[0], SUBLANES):
        rows = slice(r, r + SUBLANES)
        mu, xc, rstd = _row_stats(x_ref[rows, :])
        hb_ref[rows, :] = (xc * rstd * g + b).astype(BF16)
        mu_ref[rows, :] = jnp.broadcast_to(mu, (SUBLANES, LANES))
        rstd_ref[rows, :] = jnp.broadcast_to(rstd, (SUBLANES, LANES))


def _ln_final_kernel(x_ref, g_ref, b_ref, o_ref):
    g, b = g_ref[...], b_ref[...]
    for r in range(0, x_ref.shape[0], SUBLANES):
        rows = slice(r, r + SUBLANES)
        o_ref[rows, :] = _ln_rows(x_ref[rows, :], g, b)


def _layer_norm(x, g, b, *, final, tm=512):
    m, d = x.shape
    row = pl.BlockSpec((tm, d), lambda i: (i, 0))
    vec = pl.BlockSpec((1, d), lambda i: (0, 0))
    stat = pl.BlockSpec((tm, LANES), lambda i: (i, 0))
    if final:
        out_specs, out_shape, body = row, jax.ShapeDtypeStruct((m, d), F32), _ln_final_kernel
    else:
        out_specs = [row, stat, stat]
        out_shape = [jax.ShapeDtypeStruct((m, d), BF16), jax.ShapeDtypeStruct((m, LANES), F32),
                     jax.ShapeDtypeStruct((m, LANES), F32)]
        body = _ln_stats_kernel
    return pl.pallas_call(
        body, grid=(m // tm,), in_specs=[row, vec, vec], out_specs=out_specs, out_shape=out_shape,
        compiler_params=_cparams(("arbitrary",), 48),
        name="layer_norm_final" if final else "layer_norm",
    )(x, g.reshape(1, d), b.reshape(1, d))


IN_PROJ_SCALED = ((OFF_Q, OFF_K, Q_DIFF_SCALE), (OFF_QM, OFF_GM, MEM_HEAD_DIM ** -0.5))


def _matmul_kernel(a_ref, w_ref, o_ref, *, col_scales):
    acc = jnp.dot(a_ref[...].astype(BF16), w_ref[...].astype(BF16), preferred_element_type=F32)
    if col_scales:
        col = pl.program_id(0) * o_ref.shape[1]
        scale = jnp.float32(1.0)
        for lo, hi, s in col_scales:
            scale = jnp.where((col >= lo) & (col < hi), jnp.float32(s), scale)
        acc = acc * scale
    o_ref[...] = acc.astype(o_ref.dtype)


def _matmul_ws(a, w, layer, *, tm, tn, col_scales=(), vmem_mib, name):
    m, k = a.shape
    n = w.shape[-1]
    assert all(lo % tn == 0 and hi % tn == 0 for lo, hi, _ in col_scales)
    return pl.pallas_call(
        functools.partial(_matmul_kernel, col_scales=col_scales),
        grid=(n // tn, m // tm),
        in_specs=[pl.BlockSpec((tm, k), lambda j, i: (i, 0)),
                  pl.BlockSpec((None, k, tn), lambda j, i: (layer, 0, j))],
        out_specs=pl.BlockSpec((tm, tn), lambda j, i: (i, j)),
        out_shape=jax.ShapeDtypeStruct((m, n), BF16),
        compiler_params=_cparams(("arbitrary", "arbitrary"), vmem_mib),
        name=name,
    )(a, w)


_NT_DIMS = (((1,), (1,)), ((), ()))


def _diff_attn_kernel(q_ref, k_ref, v_ref, g_ref, lq1_ref, lk1_ref, lq2_ref, lk2_ref, sg_ref,
                      o_ref, s_ref, *, lam_init, seq):
    d = DIFF_HEAD_DIM
    lam = (jnp.exp(jnp.sum(lq1_ref[...] * lk1_ref[...], axis=-1, keepdims=True))
           - jnp.exp(jnp.sum(lq2_ref[...] * lk2_ref[...], axis=-1, keepdims=True))
           + lam_init)
    row_chunk = lax.broadcasted_iota(jnp.int32, (ATTN_TQ, ATTN_TQ), 0) // CHUNK
    col_chunk = lax.broadcasted_iota(jnp.int32, (ATTN_TQ, ATTN_TQ), 1) // CHUNK
    mask = col_chunk <= row_chunk
    out_gain = sg_ref[...] * (1.0 - lam_init)
    row_max = {}

    def scores(t):
        lo = t * ATTN_TQ
        for m in range(2):
            cols = slice(m * d, (m + 1) * d)
            q = q_ref[lo:lo + ATTN_TQ, cols]
            s_d = lax.dot_general(q, k_ref[lo:lo + ATTN_TQ, cols], _NT_DIMS, preferred_element_type=F32)
            s_d = jnp.where(mask, s_d, -jnp.inf)
            s_ref[t % 2, m, :, lo:lo + ATTN_TQ] = s_d
            mx = jnp.max(s_d, axis=-1, keepdims=True)
            if lo:
                s_o = lax.dot_general(q, k_ref[0:lo, cols], _NT_DIMS, preferred_element_type=F32)
                s_ref[t % 2, m, :, 0:lo] = s_o
                mx = jnp.maximum(mx, jnp.max(s_o, axis=-1, keepdims=True))
            row_max[(t, m)] = mx

    def outputs(t):
        lo = t * ATTN_TQ
        kv = lo + ATTN_TQ
        p = []
        for m in range(2):
            e = jnp.exp2(s_ref[t % 2, m, :, 0:kv] - row_max[(t, m)])
            l = jnp.sum(e, axis=-1, keepdims=True)
            p.append(e * ((lam if m else 1.0) / l))
        o = jnp.dot((p[0] - p[1]).astype(BF16), v_ref[0:kv, :], preferred_element_type=F32)
        o = o * lax.rsqrt(jnp.mean(o * o, axis=-1, keepdims=True) + RMS_EPS) * out_gain
        o_ref[lo:lo + ATTN_TQ, :] = (o * _silu(g_ref[lo:lo + ATTN_TQ, :].astype(F32))).astype(o_ref.dtype)

    n_tiles = seq // ATTN_TQ
    for t in range(n_tiles, -1, -1):
        if t >= 1:
            scores(t - 1)
        if t < n_tiles:
            outputs(t)


def _diff_attention(proj, lq1, lk1, lq2, lk2, subln_g, layer, *, batch, seq, lam_init):
    w = DIFF_V_DIM
    head_block = lambda off: pl.BlockSpec((seq, w), lambda b, h: (b, off // w + h))
    lam_spec = pl.BlockSpec((None, 1, DIFF_HEAD_DIM), lambda b, h: (layer, 0, 0))
    return pl.pallas_call(
        functools.partial(_diff_attn_kernel, lam_init=lam_init, seq=seq),
        grid=(batch, DIFF_HEADS),
        in_specs=[head_block(OFF_Q), head_block(OFF_K), head_block(OFF_V), head_block(OFF_GD),
                  lam_spec, lam_spec, lam_spec, lam_spec,
                  pl.BlockSpec((None, 1, DIFF_V_DIM), lambda b, h: (layer, 0, 0))],
        out_specs=pl.BlockSpec((seq, w), lambda b, h: (b, h)),
        out_shape=jax.ShapeDtypeStruct((batch * seq, DIFF_WIDTH), BF16),
        scratch_shapes=[pltpu.VMEM((2, 2, ATTN_TQ, seq), F32)],
        compiler_params=_cparams(("arbitrary", "arbitrary"), 48),
        name="diff_attention",
    )(proj, proj, proj, proj,
      lq1.reshape(DEPTH, 1, -1), lk1.reshape(DEPTH, 1, -1), lq2.reshape(DEPTH, 1, -1),
      lk2.reshape(DEPTH, 1, -1), subln_g.reshape(DEPTH, 1, -1))


CONV_TS = 256
CONV_RC = 32
CONV_PAD = CONV_HALO - (CONV_K - 1)


def _conv_kernel(a_ref, gt_ref, ah_ref, gh_ref, gc_ref, cw_ref, cb_ref, lg_ref, lb_ref, wpw_ref,
                 o_ref, hbuf_ref, shift_ref, cout_ref, wtap_ref, wpw_bf_ref, *, tiles_per_seq):
    i = pl.program_id(0)

    @pl.when(i == 0)
    def _():
        wpw_bf_ref[...] = wpw_ref[...].astype(BF16)
        for j in range(CONV_K):
            wtap_ref[j * SUBLANES:(j + 1) * SUBLANES, :] = jnp.broadcast_to(
                cw_ref[j:j + 1, :], (SUBLANES, CONV_WIDTH))

    halo = ah_ref[...].astype(F32) * jax.nn.sigmoid(gh_ref[...].astype(F32))
    keep = (i % tiles_per_seq != 0).astype(F32)
    hbuf_ref[0:CONV_HALO, :] = halo * keep
    hbuf_ref[CONV_HALO:CONV_HALO + CONV_TS, :] = (
        a_ref[...].astype(F32) * jax.nn.sigmoid(gt_ref[...].astype(F32)))

    n_shifted = CONV_HALO + CONV_TS - SUBLANES
    for k in range(1, SUBLANES):
        shift_ref[k - 1, 0:n_shifted, :] = hbuf_ref[k:k + n_shifted, :]
    n_sub = CONV_RC // SUBLANES
    for r in range(CONV_TS // CONV_RC):
        acc = [jnp.zeros((SUBLANES, CONV_WIDTH), F32) for _ in range(n_sub)]
        for j in range(CONV_K):
            k, base = (CONV_PAD + j) % SUBLANES, (CONV_PAD + j) // SUBLANES * SUBLANES
            wj = wtap_ref[j * SUBLANES:(j + 1) * SUBLANES, :]
            for s in range(n_sub):
                row = base + r * CONV_RC + s * SUBLANES
                win = shift_ref[k - 1, row:row + SUBLANES, :] if k else hbuf_ref[row:row + SUBLANES, :]
                acc[s] = acc[s] + wj * win
        for s in range(n_sub):
            row = r * CONV_RC + s * SUBLANES
            cout_ref[row:row + SUBLANES, :] = acc[s]

    c = cout_ref[...] + cb_ref[...]
    c = _silu(_ln_rows(c, lg_ref[...], lb_ref[...]))
    y = jnp.dot(c.astype(BF16), wpw_bf_ref[...], preferred_element_type=F32)
    o_ref[...] = (y * _silu(gc_ref[...].astype(F32))).astype(o_ref.dtype)


def _conv_branch(proj, conv_w, conv_b, ln_g, ln_b, w_pw, layer, *, seq):
    m = proj.shape[0]
    cw = CONV_WIDTH
    halo_per_tile = CONV_TS // CONV_HALO
    main = lambda off: pl.BlockSpec((CONV_TS, cw), lambda i: (i, off // cw))
    halo = lambda off: pl.BlockSpec(
        (CONV_HALO, cw), lambda i: (jnp.maximum(i * halo_per_tile - 1, 0), off // cw))
    vec = pl.BlockSpec((None, 1, cw), lambda i: (layer, 0, 0))
    return pl.pallas_call(
        functools.partial(_conv_kernel, tiles_per_seq=seq // CONV_TS),
        grid=(m // CONV_TS,),
        in_specs=[main(OFF_GLU_A), main(OFF_GLU_G), halo(OFF_GLU_A), halo(OFF_GLU_G), main(OFF_GC),
                  pl.BlockSpec((None, CONV_K, cw), lambda i: (layer, 0, 0)),
                  vec, vec, vec,
                  pl.BlockSpec((None, cw, cw), lambda i: (layer, 0, 0))],
        out_specs=pl.BlockSpec((CONV_TS, cw), lambda i: (i, 0)),
        out_shape=jax.ShapeDtypeStruct((m, cw), BF16),
        scratch_shapes=[pltpu.VMEM((CONV_HALO + CONV_TS, cw), F32),
                        pltpu.VMEM((SUBLANES - 1, CONV_HALO + CONV_TS, cw), F32),
                        pltpu.VMEM((CONV_TS, cw), F32),
                        pltpu.VMEM((CONV_K * SUBLANES, cw), F32),
                        pltpu.VMEM((cw, cw), BF16)],
        compiler_params=_cparams(("arbitrary",), 48),
        name="conv_branch",
    )(proj, proj, proj, proj, proj, conv_w, conv_b.reshape(DEPTH, 1, cw),
      ln_g.reshape(DEPTH, 1, cw), ln_b.reshape(DEPTH, 1, cw), w_pw)


MEM_TQ = 512


def _mem_attn_kernel(q_ref, k_ref, v_ref, g_ref, o_ref):
    for h in range(MEM_HEADS):
        cols = slice(h * MEM_HEAD_DIM, (h + 1) * MEM_HEAD_DIM)
        s = lax.dot_general(q_ref[:, cols], k_ref[:, cols], _NT_DIMS, preferred_element_type=F32)
        p = jnp.exp(s - jnp.max(s, axis=-1, keepdims=True))
        l = jnp.sum(p, axis=-1, keepdims=True)
        o = jnp.dot(p.astype(BF16), v_ref[:, cols], preferred_element_type=F32) * (1.0 / l)
        o_ref[:, cols] = (o * _silu(g_ref[:, cols].astype(F32))).astype(o_ref.dtype)


def _mem_kv(mem, w_mem_kv, *, tn=512):
    m, k = mem.shape
    n = w_mem_kv.shape[-1]
    per_layer = n // tn
    return pl.pallas_call(
        functools.partial(_matmul_kernel, col_scales=()),
        grid=(DEPTH * per_layer,),
        in_specs=[pl.BlockSpec((m, k), lambda j: (0, 0), pipeline_mode=pl.Buffered(1)),
                  pl.BlockSpec((None, k, tn), lambda j: (j // per_layer, 0, j % per_layer))],
        out_specs=pl.BlockSpec((m, tn), lambda j: (0, j)),
        out_shape=jax.ShapeDtypeStruct((m, DEPTH * n), BF16),
        compiler_params=_cparams(("arbitrary",), 52),
        name="mem_kv",
    )(mem, w_mem_kv)


def _mem_attention(proj, kv, layer, *, batch, seq, n_mem):
    w = MEM_WIDTH
    tq = seq // MEM_TQ
    return pl.pallas_call(
        _mem_attn_kernel,
        grid=(batch, tq),
        in_specs=[pl.BlockSpec((MEM_TQ, w), lambda b, t: (b * tq + t, OFF_QM // w)),
                  pl.BlockSpec((n_mem, w), lambda b, t: (b, 2 * layer)),
                  pl.BlockSpec((n_mem, w), lambda b, t: (b, 2 * layer + 1)),
                  pl.BlockSpec((MEM_TQ, w), lambda b, t: (b * tq + t, OFF_GM // w))],
        out_specs=pl.BlockSpec((MEM_TQ, w), lambda b, t: (b * tq + t, 0)),
        out_shape=jax.ShapeDtypeStruct((batch * seq, MEM_WIDTH), BF16),
        compiler_params=_cparams(("arbitrary", "arbitrary"), 32),
        name="mem_attention",
    )(proj, kv, kv, proj)


def _out_proj_kernel(yd_ref, yc_ref, ym_ref, w_ref, zp_ref, mu_ref, rstd_ref, g_ref, b_ref, z_ref):
    c0, c1 = DIFF_WIDTH, DIFF_WIDTH + CONV_WIDTH
    acc = jnp.dot(yd_ref[...], w_ref[0:c0, :].astype(BF16), preferred_element_type=F32)
    acc = acc + jnp.dot(yc_ref[...], w_ref[c0:c1, :].astype(BF16), preferred_element_type=F32)
    acc = acc + jnp.dot(ym_ref[...], w_ref[c1:D_MODEL, :].astype(BF16), preferred_element_type=F32)
    tn = z_ref.shape[1]
    h = ((zp_ref[...] - _lane_tile(mu_ref[...], tn)) * _lane_tile(rstd_ref[...], tn) * g_ref[...]
         + b_ref[...])
    z_ref[...] = DEEPNORM_ALPHA * h + acc


def _out_proj(yd, yc, ym, w_out, layer, z_prev, mu, rstd, g_prev, b_prev, *, tm=512, tn=1024):
    m = z_prev.shape[0]
    piece = pl.BlockSpec((tm, tn), lambda j, i: (i, j))
    stat = pl.BlockSpec((tm, LANES), lambda j, i: (i, 0))
    vec = pl.BlockSpec((1, tn), lambda j, i: (0, j))
    return pl.pallas_call(
        _out_proj_kernel,
        grid=(D_MODEL // tn, m // tm),
        in_specs=[pl.BlockSpec((tm, DIFF_WIDTH), lambda j, i: (i, 0)),
                  pl.BlockSpec((tm, CONV_WIDTH), lambda j, i: (i, 0)),
                  pl.BlockSpec((tm, MEM_WIDTH), lambda j, i: (i, 0)),
                  pl.BlockSpec((None, D_MODEL, tn), lambda j, i: (layer, 0, j)),
                  piece, stat, stat, vec, vec],
        out_specs=piece,
        out_shape=jax.ShapeDtypeStruct((m, D_MODEL), F32),
        compiler_params=_cparams(("arbitrary", "arbitrary"), 56),
        name="out_proj",
    )(yd, yc, ym, w_out, z_prev, mu, rstd, g_prev.reshape(1, D_MODEL), b_prev.reshape(1, D_MODEL))


def kernel(x, mem, ln_in_g, ln_in_b, w_in, lambda_q1, lambda_k1, lambda_q2, lambda_k2, subln_g,
           conv_w, conv_b, conv_ln_g, conv_ln_b, w_conv_pw, w_mem_kv, w_out, ln_post_g, ln_post_b):
    batch, seq, d = x.shape
    n_mem = mem.shape[1]
    assert d == D_MODEL and seq % ATTN_TQ == 0 and seq % CONV_TS == 0 and seq % MEM_TQ == 0
    z = x.reshape(batch * seq, d)
    ln_g, ln_b = ln_in_g, ln_in_b
    kv = _mem_kv(mem.reshape(batch * n_mem, d), w_mem_kv)
    for layer in range(DEPTH):
        lam_init = 0.8 - 0.6 * math.exp(-0.3 * layer)
        h_bf16, mu, rstd = _layer_norm(z, ln_g, ln_b, final=False)
        proj = _matmul_ws(h_bf16, w_in, layer, tm=2048, tn=512, col_scales=IN_PROJ_SCALED, vmem_mib=58,
                          name="in_proj")
        yd = _diff_attention(proj, lambda_q1, lambda_k1, lambda_q2, lambda_k2, subln_g, layer,
                             batch=batch, seq=seq, lam_init=lam_init)
        yc = _conv_branch(proj, conv_w, conv_b, conv_ln_g, conv_ln_b, w_conv_pw, layer, seq=seq)
        ym = _mem_attention(proj, kv, layer, batch=batch, seq=seq, n_mem=n_mem)
        z = _out_proj(yd, yc, ym, w_out, layer, z, mu, rstd, ln_g, ln_b)
        ln_g, ln_b = ln_post_g[layer], ln_post_b[layer]
    return _layer_norm(z, ln_g, ln_b, final=True).reshape(batch, seq, d)
```

```python
import functools
import math

import jax
import jax.numpy as jnp
from jax import lax
from jax.experimental import pallas as pl
from jax.experimental.pallas import tpu as pltpu

D_MODEL = 4096
DEPTH = 2
CHUNK = 64
DIFF_WIDTH = D_MODEL // 2
CONV_WIDTH = D_MODEL // 4
MEM_WIDTH = D_MODEL // 4
DIFF_HEADS = 8
DIFF_HEAD_DIM = DIFF_WIDTH // (2 * DIFF_HEADS)
DIFF_V_DIM = 2 * DIFF_HEAD_DIM
MEM_HEADS = 4
MEM_HEAD_DIM = MEM_WIDTH // MEM_HEADS
CONV_K = 31
LN_EPS = 1e-5
RMS_EPS = 1e-5
DEEPNORM_ALPHA = (2.0 * DEPTH) ** 0.25

OFF_Q = 0
OFF_K = OFF_Q + DIFF_WIDTH
OFF_V = OFF_K + DIFF_WIDTH
OFF_GD = OFF_V + DIFF_WIDTH
OFF_GLU_A = OFF_GD + DIFF_WIDTH
OFF_GLU_G = OFF_GLU_A + CONV_WIDTH
OFF_GC = OFF_GLU_G + CONV_WIDTH
OFF_QM = OFF_GC + CONV_WIDTH
OFF_GM = OFF_QM + MEM_WIDTH
IN_COLS = OFF_GM + MEM_WIDTH

LOG2E = 1.4426950408889634
Q_DIFF_SCALE = DIFF_HEAD_DIM ** -0.5 * LOG2E

LANES = 128
SUBLANES = 8
CONV_HALO = 32
ATTN_TQ = 256
MIB = 1024 * 1024

F32 = jnp.float32
BF16 = jnp.bfloat16


def _cparams(sem, vmem_mib):
    return pltpu.CompilerParams(dimension_semantics=sem, vmem_limit_bytes=int(vmem_mib * MIB))


def _silu(x):
    return x * jax.nn.sigmoid(x)


def _lane_tile(v, width):
    return jnp.concatenate([v] * (width // LANES), axis=1)


def _row_stats(x):
    mu = jnp.mean(x, axis=-1, keepdims=True)
    xc = x - mu
    var = jnp.mean(xc * xc, axis=-1, keepdims=True)
    return mu, xc, lax.rsqrt(var + LN_EPS)


def _ln_rows(x, g, b):
    _, xc, rstd = _row_stats(x)
    return xc * rstd * g + b


def _ln_stats_kernel(x_ref, g_ref, b_ref, hb_ref, mu_ref, rstd_ref):
    g, b = g_ref[...], b_ref[...]
    for r in range(0, x_ref.shape[0], SUBLANES):
        rows = slice(r, r + SUBLANES)
        mu, xc, rstd = _row_stats(x_ref[rows, :])
        hb_ref[rows, :] = (xc * rstd * g + b).astype(BF16)
        mu_ref[rows, :] = jnp.broadcast_to(mu, (SUBLANES, LANES))
        rstd_ref[rows, :] = jnp.broadcast_to(rstd, (SUBLANES, LANES))


def _ln_final_kernel(x_ref, g_ref, b_ref, o_ref):
    g, b = g_ref[...], b_ref[...]
    for r in range(0, x_ref.shape[0], SUBLANES):
        rows = slice(r, r + SUBLANES)
        o_ref[rows, :] = _ln_rows(x_ref[rows, :], g, b)


def _layer_norm(x, g, b, *, final, tm=512):
    m, d = x.shape
    row = pl.BlockSpec((tm, d), lambda i: (i, 0))
    vec = pl.BlockSpec((1, d), lambda i: (0, 0))
    stat = pl.BlockSpec((tm, LANES), lambda i: (i, 0))
    if final:
        out_specs, out_shape, body = row, jax.ShapeDtypeStruct((m, d), F32), _ln_final_kernel
    else:
        out_specs = [row, stat, stat]
        out_shape = [jax.ShapeDtypeStruct((m, d), BF16), jax.ShapeDtypeStruct((m, LANES), F32),
                     jax.ShapeDtypeStruct((m, LANES), F32)]
        body = _ln_stats_kernel
    return pl.pallas_call(
        body, grid=(m // tm,), in_specs=[row, vec, vec], out_specs=out_specs, out_shape=out_shape,
        compiler_params=_cparams(("arbitrary",), 48),
        name="layer_norm_final" if final else "layer_norm",
    )(x, g.reshape(1, d), b.reshape(1, d))


IN_PROJ_SCALED = ((OFF_Q, OFF_K, Q_DIFF_SCALE), (OFF_QM, OFF_GM, MEM_HEAD_DIM ** -0.5))


def _matmul_kernel(a_ref, w_ref, o_ref, *, col_axis, col_scales):
    acc = jnp.dot(a_ref[...].astype(BF16), w_ref[...].astype(BF16), preferred_element_type=F32)
    if col_scales:
        col = pl.program_id(col_axis) * o_ref.shape[1]
        scale = jnp.float32(1.0)
        for lo, hi, s in col_scales:
            scale = jnp.where((col >= lo) & (col < hi), jnp.float32(s), scale)
        acc = acc * scale
    o_ref[...] = acc.astype(o_ref.dtype)


def _in_proj(h, w_in, layer, *, tm=2048, tn=512):
    m, k = h.shape
    n = w_in.shape[-1]
    assert all(lo % tn == 0 and hi % tn == 0 for lo, hi, _ in IN_PROJ_SCALED)
    return pl.pallas_call(
        functools.partial(_matmul_kernel, col_axis=1, col_scales=IN_PROJ_SCALED),
        grid=(m // tm, n // tn),
        in_specs=[pl.BlockSpec((tm, k), lambda i, j: (i, 0)),
                  pl.BlockSpec((None, k, tn), lambda i, j: (layer, 0, j))],
        out_specs=pl.BlockSpec((tm, tn), lambda i, j: (i, j)),
        out_shape=jax.ShapeDtypeStruct((m, n), BF16),
        compiler_params=_cparams(("arbitrary", "arbitrary"), 58),
        name="in_proj",
    )(h, w_in)


_NT_DIMS = (((1,), (1,)), ((), ()))


def _diff_attn_kernel(q_ref, k_ref, v_ref, g_ref, lq1_ref, lk1_ref, lq2_ref, lk2_ref, sg_ref,
                      o_ref, s_ref, *, lam_init, seq):
    d = DIFF_HEAD_DIM
    lam = (jnp.exp(jnp.sum(lq1_ref[...] * lk1_ref[...], axis=-1, keepdims=True))
           - jnp.exp(jnp.sum(lq2_ref[...] * lk2_ref[...], axis=-1, keepdims=True))
           + lam_init)
    row_chunk = lax.broadcasted_iota(jnp.int32, (ATTN_TQ, ATTN_TQ), 0) // CHUNK
    col_chunk = lax.broadcasted_iota(jnp.int32, (ATTN_TQ, ATTN_TQ), 1) // CHUNK
    mask = col_chunk <= row_chunk
    out_gain = sg_ref[...] * (1.0 - lam_init)
    row_max = {}

    def scores(t):
        lo = t * ATTN_TQ
        for m in range(2):
            cols = slice(m * d, (m + 1) * d)
            q = q_ref[lo:lo + ATTN_TQ, cols]
            s_d = lax.dot_general(q, k_ref[lo:lo + ATTN_TQ, cols], _NT_DIMS, preferred_element_type=F32)
            s_d = jnp.where(mask, s_d, -jnp.inf)
            s_ref[t % 2, m, :, lo:lo + ATTN_TQ] = s_d
            mx = jnp.max(s_d, axis=-1, keepdims=True)
            if lo:
                s_o = lax.dot_general(q, k_ref[0:lo, cols], _NT_DIMS, preferred_element_type=F32)
                s_ref[t % 2, m, :, 0:lo] = s_o
                mx = jnp.maximum(mx, jnp.max(s_o, axis=-1, keepdims=True))
            row_max[(t, m)] = mx

    def outputs(t):
        lo = t * ATTN_TQ
        kv = lo + ATTN_TQ
        p = []
        for m in range(2):
            e = jnp.exp2(s_ref[t % 2, m, :, 0:kv] - row_max[(t, m)])
            l = jnp.sum(e, axis=-1, keepdims=True)
            p.append(e * ((lam if m else 1.0) / l))
        o = jnp.dot((p[0] - p[1]).astype(BF16), v_ref[0:kv, :], preferred_element_type=F32)
        o = o * lax.rsqrt(jnp.mean(o * o, axis=-1, keepdims=True) + RMS_EPS) * out_gain
        o_ref[lo:lo + ATTN_TQ, :] = (o * _silu(g_ref[lo:lo + ATTN_TQ, :].astype(F32))).astype(o_ref.dtype)

    n_tiles = seq // ATTN_TQ
    for t in range(n_tiles, -1, -1):
        if t >= 1:
            scores(t - 1)
        if t < n_tiles:
            outputs(t)


def _diff_attention(proj, lq1, lk1, lq2, lk2, subln_g, layer, *, batch, seq, lam_init):
    w = DIFF_V_DIM
    head_block = lambda off: pl.BlockSpec((seq, w), lambda b, h: (b, off // w + h))
    lam_spec = pl.BlockSpec((None, 1, DIFF_HEAD_DIM), lambda b, h: (layer, 0, 0))
    return pl.pallas_call(
        functools.partial(_diff_attn_kernel, lam_init=lam_init, seq=seq),
        grid=(batch, DIFF_HEADS),
        in_specs=[head_block(OFF_Q), head_block(OFF_K), head_block(OFF_V), head_block(OFF_GD),
                  lam_spec, lam_spec, lam_spec, lam_spec,
                  pl.BlockSpec((None, 1, DIFF_V_DIM), lambda b, h: (layer, 0, 0))],
        out_specs=pl.BlockSpec((seq, w), lambda b, h: (b, h)),
        out_shape=jax.ShapeDtypeStruct((batch * seq, DIFF_WIDTH), BF16),
        scratch_shapes=[pltpu.VMEM((2, 2, ATTN_TQ, seq), F32)],
        compiler_params=_cparams(("arbitrary", "arbitrary"), 48),
        name="diff_attention",
    )(proj, proj, proj, proj,
      lq1.reshape(DEPTH, 1, -1), lk1.reshape(DEPTH, 1, -1), lq2.reshape(DEPTH, 1, -1),
      lk2.reshape(DEPTH, 1, -1), subln_g.reshape(DEPTH, 1, -1))


CONV_TS = 512
CONV_RC = 64
CONV_PAD = CONV_HALO - (CONV_K - 1)


def _conv_kernel(a_ref, gt_ref, ah_ref, gh_ref, gc_ref, cw_ref, cb_ref, lg_ref, lb_ref, wpw_ref,
                 o_ref, hbuf_ref, shift_ref, cout_ref, wtap_ref, wpw_bf_ref, *, tiles_per_seq):
    i = pl.program_id(0)

    @pl.when(i == 0)
    def _():
        wpw_bf_ref[...] = wpw_ref[...].astype(BF16)
        for j in range(CONV_K):
            wtap_ref[j * SUBLANES:(j + 1) * SUBLANES, :] = jnp.broadcast_to(
                cw_ref[j:j + 1, :], (SUBLANES, CONV_WIDTH))

    halo = ah_ref[...].astype(F32) * jax.nn.sigmoid(gh_ref[...].astype(F32))
    keep = (i % tiles_per_seq != 0).astype(F32)
    hbuf_ref[0:CONV_HALO, :] = halo * keep
    hbuf_ref[CONV_HALO:CONV_HALO + CONV_TS, :] = (
        a_ref[...].astype(F32) * jax.nn.sigmoid(gt_ref[...].astype(F32)))

    n_shifted = CONV_HALO + CONV_TS - SUBLANES
    for k in range(1, SUBLANES):
        shift_ref[k - 1, 0:n_shifted, :] = hbuf_ref[k:k + n_shifted, :]
    n_sub = CONV_RC // SUBLANES
    for r in range(CONV_TS // CONV_RC):
        acc = [jnp.zeros((SUBLANES, CONV_WIDTH), F32) for _ in range(n_sub)]
        for j in range(CONV_K):
            k, base = (CONV_PAD + j) % SUBLANES, (CONV_PAD + j) // SUBLANES * SUBLANES
            wj = wtap_ref[j * SUBLANES:(j + 1) * SUBLANES, :]
            for s in range(n_sub):
                row = base + r * CONV_RC + s * SUBLANES
                win = shift_ref[k - 1, row:row + SUBLANES, :] if k else hbuf_ref[row:row + SUBLANES, :]
                acc[s] = acc[s] + wj * win
        for s in range(n_sub):
            row = r * CONV_RC + s * SUBLANES
            cout_ref[row:row + SUBLANES, :] = acc[s]

    c = cout_ref[...] + cb_ref[...]
    c = _silu(_ln_rows(c, lg_ref[...], lb_ref[...]))
    y = jnp.dot(c.astype(BF16), wpw_bf_ref[...], preferred_element_type=F32)
    o_ref[...] = (y * _silu(gc_ref[...].astype(F32))).astype(o_ref.dtype)


def _conv_branch(proj, conv_w, conv_b, ln_g, ln_b, w_pw, layer, *, seq):
    m = proj.shape[0]
    cw = CONV_WIDTH
    halo_per_tile = CONV_TS // CONV_HALO
    main = lambda off: pl.BlockSpec((CONV_TS, cw), lambda i: (i, off // cw))
    halo = lambda off: pl.BlockSpec(
        (CONV_HALO, cw), lambda i: (jnp.maximum(i * halo_per_tile - 1, 0), off // cw))
    vec = pl.BlockSpec((None, 1, cw), lambda i: (layer, 0, 0))
    return pl.pallas_call(
        functools.partial(_conv_kernel, tiles_per_seq=seq // CONV_TS),
        grid=(m // CONV_TS,),
        in_specs=[main(OFF_GLU_A), main(OFF_GLU_G), halo(OFF_GLU_A), halo(OFF_GLU_G), main(OFF_GC),
                  pl.BlockSpec((None, CONV_K, cw), lambda i: (layer, 0, 0)),
                  vec, vec, vec,
                  pl.BlockSpec((None, cw, cw), lambda i: (layer, 0, 0))],
        out_specs=pl.BlockSpec((CONV_TS, cw), lambda i: (i, 0)),
        out_shape=jax.ShapeDtypeStruct((m, cw), BF16),
        scratch_shapes=[pltpu.VMEM((CONV_HALO + CONV_TS, cw), F32),
                        pltpu.VMEM((SUBLANES - 1, CONV_HALO + CONV_TS, cw), F32),
                        pltpu.VMEM((CONV_TS, cw), F32),
                        pltpu.VMEM((CONV_K * SUBLANES, cw), F32),
                        pltpu.VMEM((cw, cw), BF16)],
        compiler_params=_cparams(("arbitrary",), 48),
        name="conv_branch",
    )(proj, proj, proj, proj, proj, conv_w, conv_b.reshape(DEPTH, 1, cw),
      ln_g.reshape(DEPTH, 1, cw), ln_b.reshape(DEPTH, 1, cw), w_pw)


MEM_TQ = 2048


def _mem_attn_kernel(q_ref, k_ref, v_ref, g_ref, o_ref):
    for h in range(MEM_HEADS):
        cols = slice(h * MEM_HEAD_DIM, (h + 1) * MEM_HEAD_DIM)
        s = lax.dot_general(q_ref[:, cols], k_ref[:, cols], _NT_DIMS, preferred_element_type=F32)
        p = jnp.exp(s - jnp.max(s, axis=-1, keepdims=True))
        l = jnp.sum(p, axis=-1, keepdims=True)
        o = jnp.dot(p.astype(BF16), v_ref[:, cols], preferred_element_type=F32) * (1.0 / l)
        o_ref[:, cols] = (o * _silu(g_ref[:, cols].astype(F32))).astype(o_ref.dtype)


def _mem_kv(mem, w_mem_kv, *, tn=512):
    m, k = mem.shape
    n = w_mem_kv.shape[-1]
    per_layer = n // tn
    return pl.pallas_call(
        functools.partial(_matmul_kernel, col_axis=0, col_scales=()),
        grid=(DEPTH * per_layer,),
        in_specs=[pl.BlockSpec((m, k), lambda j: (0, 0), pipeline_mode=pl.Buffered(1)),
                  pl.BlockSpec((None, k, tn), lambda j: (j // per_layer, 0, j % per_layer))],
        out_specs=pl.BlockSpec((m, tn), lambda j: (0, j)),
        out_shape=jax.ShapeDtypeStruct((m, DEPTH * n), BF16),
        compiler_params=_cparams(("arbitrary",), 52),
        name="mem_kv",
    )(mem, w_mem_kv)


def _mem_attention(proj, kv, layer, *, batch, seq, n_mem):
    w = MEM_WIDTH
    tq = seq // MEM_TQ
    return pl.pallas_call(
        _mem_attn_kernel,
        grid=(batch, tq),
        in_specs=[pl.BlockSpec((MEM_TQ, w), lambda b, t: (b * tq + t, OFF_QM // w)),
                  pl.BlockSpec((n_mem, w), lambda b, t: (b, 2 * layer)),
                  pl.BlockSpec((n_mem, w), lambda b, t: (b, 2 * layer + 1)),
                  pl.BlockSpec((MEM_TQ, w), lambda b, t: (b * tq + t, OFF_GM // w))],
        out_specs=pl.BlockSpec((MEM_TQ, w), lambda b, t: (b * tq + t, 0)),
        out_shape=jax.ShapeDtypeStruct((batch * seq, MEM_WIDTH), BF16),
        compiler_params=_cparams(("arbitrary", "arbitrary"), 32),
        name="mem_attention",
    )(proj, kv, kv, proj)


def _out_proj_kernel(yd_ref, yc_ref, ym_ref, w_ref, zp_ref, mu_ref, rstd_ref, g_ref, b_ref, z_ref):
    c0, c1 = DIFF_WIDTH, DIFF_WIDTH + CONV_WIDTH
    acc = jnp.dot(yd_ref[...], w_ref[0:c0, :].astype(BF16), preferred_element_type=F32)
    acc = acc + jnp.dot(yc_ref[...], w_ref[c0:c1, :].astype(BF16), preferred_element_type=F32)
    acc = acc + jnp.dot(ym_ref[...], w_ref[c1:D_MODEL, :].astype(BF16), preferred_element_type=F32)
    tn = z_ref.shape[1]
    h = ((zp_ref[...] - _lane_tile(mu_ref[...], tn)) * _lane_tile(rstd_ref[...], tn) * g_ref[...]
         + b_ref[...])
    z_ref[...] = DEEPNORM_ALPHA * h + acc


def _out_proj(yd, yc, ym, w_out, layer, z_prev, mu, rstd, g_prev, b_prev, *, tm=512, tn=1024):
    m = z_prev.shape[0]
    piece = pl.BlockSpec((tm, tn), lambda j, i: (i, j))
    stat = pl.BlockSpec((tm, LANES), lambda j, i: (i, 0))
    vec = pl.BlockSpec((1, tn), lambda j, i: (0, j))
    return pl.pallas_call(
        _out_proj_kernel,
        grid=(D_MODEL // tn, m // tm),
        in_specs=[pl.BlockSpec((tm, DIFF_WIDTH), lambda j, i: (i, 0)),
                  pl.BlockSpec((tm, CONV_WIDTH), lambda j, i: (i, 0)),
                  pl.BlockSpec((tm, MEM_WIDTH), lambda j, i: (i, 0)),
                  pl.BlockSpec((None, D_MODEL, tn), lambda j, i: (layer, 0, j)),
                  piece, stat, stat, vec, vec],
        out_specs=piece,
        out_shape=jax.ShapeDtypeStruct((m, D_MODEL), F32),
        compiler_params=_cparams(("arbitrary", "arbitrary"), 56),
        name="out_proj",
    )(yd, yc, ym, w_out, z_prev, mu, rstd, g_prev.reshape(1, D_MODEL), b_prev.reshape(1, D_MODEL))


def kernel(x, mem, ln_in_g, ln_in_b, w_in, lambda_q1, lambda_k1, lambda_q2, lambda_k2, subln_g,
           conv_w, conv_b, conv_ln_g, conv_ln_b, w_conv_pw, w_mem_kv, w_out, ln_post_g, ln_post_b):
    batch, seq, d = x.shape
    n_mem = mem.shape[1]
    assert d == D_MODEL and seq % ATTN_TQ == 0 and seq % CONV_TS == 0 and seq % MEM_TQ == 0
    z = x.reshape(batch * seq, d)
    ln_g, ln_b = ln_in_g, ln_in_b
    kv = _mem_kv(mem.reshape(batch * n_mem, d), w_mem_kv)
    for layer in range(DEPTH):
        lam_init = 0.8 - 0.6 * math.exp(-0.3 * layer)
        h_bf16, mu, rstd = _layer_norm(z, ln_g, ln_b, final=False)
        proj = _in_proj(h_bf16, w_in, layer)
        yd = _diff_attention(proj, lambda_q1, lambda_k1, lambda_q2, lambda_k2, subln_g, layer,
                             batch=batch, seq=seq, lam_init=lam_init)
        yc = _conv_branch(proj, conv_w, conv_b, conv_ln_g, conv_ln_b, w_conv_pw, layer, seq=seq)
        ym = _mem_attention(proj, kv, layer, batch=batch, seq=seq, n_mem=n_mem)
        z = _out_proj(yd, yc, ym, w_out, layer, z, mu, rstd, ln_g, ln_b)
        ln_g, ln_b = ln_post_g[layer], ln_post_b[layer]
    return _layer_norm(z, ln_g, ln_b, final=True).reshape(batch, seq, d)
```

```python
import functools
import math

import jax
import jax.numpy as jnp
import numpy as np
from jax import lax
from jax.experimental import pallas as pl
from jax.experimental.pallas import tpu as pltpu

D_MODEL = 4096
DEPTH = 2
CHUNK = 64
DIFF_WIDTH = D_MODEL // 2
CONV_WIDTH = D_MODEL // 4
MEM_WIDTH = D_MODEL // 4
DIFF_HEADS = 8
DIFF_HEAD_DIM = DIFF_WIDTH // (2 * DIFF_HEADS)
DIFF_V_DIM = 2 * DIFF_HEAD_DIM
MEM_HEADS = 4
MEM_HEAD_DIM = MEM_WIDTH // MEM_HEADS
CONV_K = 31
LN_EPS = 1e-5
RMS_EPS = 1e-5
DEEPNORM_ALPHA = (2.0 * DEPTH) ** 0.25

OFF_Q = 0
OFF_K = OFF_Q + DIFF_WIDTH
OFF_V = OFF_K + DIFF_WIDTH
OFF_GD = OFF_V + DIFF_WIDTH
OFF_GLU_A = OFF_GD + DIFF_WIDTH
OFF_GLU_G = OFF_GLU_A + CONV_WIDTH
OFF_GC = OFF_GLU_G + CONV_WIDTH
OFF_QM = OFF_GC + CONV_WIDTH
OFF_GM = OFF_QM + MEM_WIDTH
IN_COLS = OFF_GM + MEM_WIDTH

LOG2E = 1.4426950408889634
Q_DIFF_SCALE = DIFF_HEAD_DIM ** -0.5 * LOG2E

LANES = 128
SUBLANES = 8
CONV_HALO = 32
ATTN_TQ = 256
MIB = 1024 * 1024

F32 = jnp.float32
BF16 = jnp.bfloat16


def _cparams(sem, vmem_mib):
    return pltpu.CompilerParams(dimension_semantics=sem, vmem_limit_bytes=int(vmem_mib * MIB))


def _silu(x):
    return x * jax.nn.sigmoid(x)


def _lane_tile(v, width):
    return jnp.concatenate([v] * (width // LANES), axis=1)


def _row_stats(x):
    mu = jnp.mean(x, axis=-1, keepdims=True)
    xc = x - mu
    var = jnp.mean(xc * xc, axis=-1, keepdims=True)
    return mu, xc, lax.rsqrt(var + LN_EPS)


def _ln_rows(x, g, b):
    _, xc, rstd = _row_stats(x)
    return xc * rstd * g + b


def _ln_stats_kernel(x_ref, g_ref, b_ref, hb_ref, mu_ref, rstd_ref):
    g, b = g_ref[...], b_ref[...]
    for r in range(0, x_ref.shape[0], SUBLANES):
        rows = slice(r, r + SUBLANES)
        mu, xc, rstd = _row_stats(x_ref[rows, :])
        hb_ref[rows, :] = (xc * rstd * g + b).astype(BF16)
        mu_ref[rows, :] = jnp.broadcast_to(mu, (SUBLANES, LANES))
        rstd_ref[rows, :] = jnp.broadcast_to(rstd, (SUBLANES, LANES))


def _ln_final_kernel(x_ref, g_ref, b_ref, o_ref):
    g, b = g_ref[...], b_ref[...]
    for r in range(0, x_ref.shape[0], SUBLANES):
        rows = slice(r, r + SUBLANES)
        o_ref[rows, :] = _ln_rows(x_ref[rows, :], g, b)


def _layer_norm(x, g, b, *, final, tm=512):
    m, d = x.shape
    row = pl.BlockSpec((tm, d), lambda i: (i, 0))
    vec = pl.BlockSpec((1, d), lambda i: (0, 0))
    stat = pl.BlockSpec((tm, LANES), lambda i: (i, 0))
    if final:
        out_specs, out_shape, body = row, jax.ShapeDtypeStruct((m, d), F32), _ln_final_kernel
    else:
        out_specs = [row, stat, stat]
        out_shape = [jax.ShapeDtypeStruct((m, d), BF16), jax.ShapeDtypeStruct((m, LANES), F32),
                     jax.ShapeDtypeStruct((m, LANES), F32)]
        body = _ln_stats_kernel
    return pl.pallas_call(
        body, grid=(m // tm,), in_specs=[row, vec, vec], out_specs=out_specs, out_shape=out_shape,
        compiler_params=_cparams(("arbitrary",), 48),
        name="layer_norm_final" if final else "layer_norm",
    )(x, g.reshape(1, d), b.reshape(1, d))


IN_PROJ_SCALED = ((OFF_Q, OFF_K, Q_DIFF_SCALE), (OFF_QM, OFF_GM, MEM_HEAD_DIM ** -0.5))


def _matmul_kernel(a_ref, w_ref, o_ref):
    acc = jnp.dot(a_ref[...].astype(BF16), w_ref[...].astype(BF16), preferred_element_type=F32)
    o_ref[...] = acc.astype(o_ref.dtype)


def _in_proj_block(a_ref, w_ref, s_ref, o_ref):
    acc = jnp.dot(a_ref[...], w_ref[...].astype(BF16), preferred_element_type=F32)
    o_ref[...] = (acc * s_ref[...]).astype(o_ref.dtype)


def _in_proj_kernel(h_hbm, w_hbm, s_hbm, o_hbm, *, layer, tm, tn):
    m, k = h_hbm.shape
    n = o_hbm.shape[1]
    pltpu.emit_pipeline(
        _in_proj_block,
        grid=(m // tm, n // tn),
        in_specs=[pl.BlockSpec((tm, k), lambda i, j: (i, 0)),
                  pl.BlockSpec((k, tn), lambda i, j: (0, j)),
                  pl.BlockSpec((1, tn), lambda i, j: (0, j))],
        out_specs=[pl.BlockSpec((tm, tn), lambda i, j: (i, j))],
    )(h_hbm, w_hbm.at[layer], s_hbm, o_hbm)


def _in_proj(h, w_in, layer, *, tm=2048, tn=512):
    m, k = h.shape
    n = w_in.shape[-1]
    assert m % tm == 0 and n % tn == 0
    col_scale = np.ones((1, n), np.float32)
    for lo, hi, scale in IN_PROJ_SCALED:
        col_scale[0, lo:hi] = scale
    return pl.pallas_call(
        functools.partial(_in_proj_kernel, layer=layer, tm=tm, tn=tn),
        in_specs=[pl.BlockSpec(memory_space=pl.ANY)] * 3,
        out_specs=pl.BlockSpec(memory_space=pl.ANY),
        out_shape=jax.ShapeDtypeStruct((m, n), BF16),
        compiler_params=pltpu.CompilerParams(vmem_limit_bytes=58 * MIB),
        name="in_proj",
    )(h, w_in, jnp.asarray(col_scale))


_NT_DIMS = (((1,), (1,)), ((), ()))


def _diff_attn_kernel(q_ref, k_ref, v_ref, g_ref, lq1_ref, lk1_ref, lq2_ref, lk2_ref, sg_ref,
                      o_ref, s_ref, *, lam_init, seq):
    d = DIFF_HEAD_DIM
    lam = (jnp.exp(jnp.sum(lq1_ref[...] * lk1_ref[...], axis=-1, keepdims=True))
           - jnp.exp(jnp.sum(lq2_ref[...] * lk2_ref[...], axis=-1, keepdims=True))
           + lam_init)
    row_chunk = lax.broadcasted_iota(jnp.int32, (ATTN_TQ, ATTN_TQ), 0) // CHUNK
    col_chunk = lax.broadcasted_iota(jnp.int32, (ATTN_TQ, ATTN_TQ), 1) // CHUNK
    mask = col_chunk <= row_chunk
    out_gain = sg_ref[...] * (1.0 - lam_init)
    row_max = {}

    def scores(t):
        lo = t * ATTN_TQ
        for m in range(2):
            cols = slice(m * d, (m + 1) * d)
            q = q_ref[lo:lo + ATTN_TQ, cols]
            s_d = lax.dot_general(q, k_ref[lo:lo + ATTN_TQ, cols], _NT_DIMS, preferred_element_type=F32)
            s_d = jnp.where(mask, s_d, -jnp.inf)
            s_ref[t % 2, m, :, lo:lo + ATTN_TQ] = s_d
            mx = jnp.max(s_d, axis=-1, keepdims=True)
            if lo:
                s_o = lax.dot_general(q, k_ref[0:lo, cols], _NT_DIMS, preferred_element_type=F32)
                s_ref[t % 2, m, :, 0:lo] = s_o
                mx = jnp.maximum(mx, jnp.max(s_o, axis=-1, keepdims=True))
            row_max[(t, m)] = mx

    def outputs(t):
        lo = t * ATTN_TQ
        kv = lo + ATTN_TQ
        p = []
        for m in range(2):
            e = jnp.exp2(s_ref[t % 2, m, :, 0:kv] - row_max[(t, m)])
            l = jnp.sum(e, axis=-1, keepdims=True)
            p.append(e * ((lam if m else 1.0) / l))
        o = jnp.dot((p[0] - p[1]).astype(BF16), v_ref[0:kv, :], preferred_element_type=F32)
        o = o * lax.rsqrt(jnp.mean(o * o, axis=-1, keepdims=True) + RMS_EPS) * out_gain
        o_ref[lo:lo + ATTN_TQ, :] = (o * _silu(g_ref[lo:lo + ATTN_TQ, :].astype(F32))).astype(o_ref.dtype)

    n_tiles = seq // ATTN_TQ
    for t in range(n_tiles, -1, -1):
        if t >= 1:
            scores(t - 1)
        if t < n_tiles:
            outputs(t)


def _diff_attention(proj, lq1, lk1, lq2, lk2, subln_g, layer, *, batch, seq, lam_init):
    w = DIFF_V_DIM
    head_block = lambda off: pl.BlockSpec((seq, w), lambda b, h: (b, off // w + h))
    lam_spec = pl.BlockSpec((None, 1, DIFF_HEAD_DIM), lambda b, h: (layer, 0, 0))
    return pl.pallas_call(
        functools.partial(_diff_attn_kernel, lam_init=lam_init, seq=seq),
        grid=(batch, DIFF_HEADS),
        in_specs=[head_block(OFF_Q), head_block(OFF_K), head_block(OFF_V), head_block(OFF_GD),
                  lam_spec, lam_spec, lam_spec, lam_spec,
                  pl.BlockSpec((None, 1, DIFF_V_DIM), lambda b, h: (layer, 0, 0))],
        out_specs=pl.BlockSpec((seq, w), lambda b, h: (b, h)),
        out_shape=jax.ShapeDtypeStruct((batch * seq, DIFF_WIDTH), BF16),
        scratch_shapes=[pltpu.VMEM((2, 2, ATTN_TQ, seq), F32)],
        compiler_params=_cparams(("arbitrary", "arbitrary"), 48),
        name="diff_attention",
    )(proj, proj, proj, proj,
      lq1.reshape(DEPTH, 1, -1), lk1.reshape(DEPTH, 1, -1), lq2.reshape(DEPTH, 1, -1),
      lk2.reshape(DEPTH, 1, -1), subln_g.reshape(DEPTH, 1, -1))


CONV_TS = 512
CONV_RC = 64
CONV_PAD = CONV_HALO - (CONV_K - 1)


def _conv_kernel(a_ref, gt_ref, ah_ref, gh_ref, gc_ref, cw_ref, cb_ref, lg_ref, lb_ref, wpw_ref,
                 o_ref, hbuf_ref, shift_ref, cout_ref, wtap_ref, wpw_bf_ref, *, tiles_per_seq):
    i = pl.program_id(0)

    @pl.when(i == 0)
    def _():
        wpw_bf_ref[...] = wpw_ref[...].astype(BF16)
        for j in range(CONV_K):
            wtap_ref[j * SUBLANES:(j + 1) * SUBLANES, :] = jnp.broadcast_to(
                cw_ref[j:j + 1, :], (SUBLANES, CONV_WIDTH))

    halo = ah_ref[...].astype(F32) * jax.nn.sigmoid(gh_ref[...].astype(F32))
    keep = (i % tiles_per_seq != 0).astype(F32)
    hbuf_ref[0:CONV_HALO, :] = halo * keep
    hbuf_ref[CONV_HALO:CONV_HALO + CONV_TS, :] = (
        a_ref[...].astype(F32) * jax.nn.sigmoid(gt_ref[...].astype(F32)))

    n_shifted = CONV_HALO + CONV_TS - SUBLANES
    for k in range(1, SUBLANES):
        shift_ref[k - 1, 0:n_shifted, :] = hbuf_ref[k:k + n_shifted, :]
    n_sub = CONV_RC // SUBLANES
    for r in range(CONV_TS // CONV_RC):
        acc = [jnp.zeros((SUBLANES, CONV_WIDTH), F32) for _ in range(n_sub)]
        for j in range(CONV_K):
            k, base = (CONV_PAD + j) % SUBLANES, (CONV_PAD + j) // SUBLANES * SUBLANES
            wj = wtap_ref[j * SUBLANES:(j + 1) * SUBLANES, :]
            for s in range(n_sub):
                row = base + r * CONV_RC + s * SUBLANES
                win = shift_ref[k - 1, row:row + SUBLANES, :] if k else hbuf_ref[row:row + SUBLANES, :]
                acc[s] = acc[s] + wj * win
        for s in range(n_sub):
            row = r * CONV_RC + s * SUBLANES
            cout_ref[row:row + SUBLANES, :] = acc[s]

    c = cout_ref[...] + cb_ref[...]
    c = _silu(_ln_rows(c, lg_ref[...], lb_ref[...]))
    y = jnp.dot(c.astype(BF16), wpw_bf_ref[...], preferred_element_type=F32)
    o_ref[...] = (y * _silu(gc_ref[...].astype(F32))).astype(o_ref.dtype)


def _conv_branch(proj, conv_w, conv_b, ln_g, ln_b, w_pw, layer, *, seq):
    m = proj.shape[0]
    cw = CONV_WIDTH
    halo_per_tile = CONV_TS // CONV_HALO
    main = lambda off: pl.BlockSpec((CONV_TS, cw), lambda i: (i, off // cw))
    halo = lambda off: pl.BlockSpec(
        (CONV_HALO, cw), lambda i: (jnp.maximum(i * halo_per_tile - 1, 0), off // cw))
    vec = pl.BlockSpec((None, 1, cw), lambda i: (layer, 0, 0))
    return pl.pallas_call(
        functools.partial(_conv_kernel, tiles_per_seq=seq // CONV_TS),
        grid=(m // CONV_TS,),
        in_specs=[main(OFF_GLU_A), main(OFF_GLU_G), halo(OFF_GLU_A), halo(OFF_GLU_G), main(OFF_GC),
                  pl.BlockSpec((None, CONV_K, cw), lambda i: (layer, 0, 0)),
                  vec, vec, vec,
                  pl.BlockSpec((None, cw, cw), lambda i: (layer, 0, 0))],
        out_specs=pl.BlockSpec((CONV_TS, cw), lambda i: (i, 0)),
        out_shape=jax.ShapeDtypeStruct((m, cw), BF16),
        scratch_shapes=[pltpu.VMEM((CONV_HALO + CONV_TS, cw), F32),
                        pltpu.VMEM((SUBLANES - 1, CONV_HALO + CONV_TS, cw), F32),
                        pltpu.VMEM((CONV_TS, cw), F32),
                        pltpu.VMEM((CONV_K * SUBLANES, cw), F32),
                        pltpu.VMEM((cw, cw), BF16)],
        compiler_params=_cparams(("arbitrary",), 48),
        name="conv_branch",
    )(proj, proj, proj, proj, proj, conv_w, conv_b.reshape(DEPTH, 1, cw),
      ln_g.reshape(DEPTH, 1, cw), ln_b.reshape(DEPTH, 1, cw), w_pw)


MEM_TQ = 2048


def _mem_attn_kernel(q_ref, k_ref, v_ref, g_ref, o_ref):
    for h in range(MEM_HEADS):
        cols = slice(h * MEM_HEAD_DIM, (h + 1) * MEM_HEAD_DIM)
        s = lax.dot_general(q_ref[:, cols], k_ref[:, cols], _NT_DIMS, preferred_element_type=F32)
        p = jnp.exp(s - jnp.max(s, axis=-1, keepdims=True))
        l = jnp.sum(p, axis=-1, keepdims=True)
        o = jnp.dot(p.astype(BF16), v_ref[:, cols], preferred_element_type=F32) * (1.0 / l)
        o_ref[:, cols] = (o * _silu(g_ref[:, cols].astype(F32))).astype(o_ref.dtype)


def _mem_kv(mem, w_mem_kv, *, tn=512):
    m, k = mem.shape
    n = w_mem_kv.shape[-1]
    per_layer = n // tn
    return pl.pallas_call(
        _matmul_kernel,
        grid=(DEPTH * per_layer,),
        in_specs=[pl.BlockSpec((m, k), lambda j: (0, 0), pipeline_mode=pl.Buffered(1)),
                  pl.BlockSpec((None, k, tn), lambda j: (j // per_layer, 0, j % per_layer))],
        out_specs=pl.BlockSpec((m, tn), lambda j: (0, j)),
        out_shape=jax.ShapeDtypeStruct((m, DEPTH * n), BF16),
        compiler_params=_cparams(("arbitrary",), 52),
        name="mem_kv",
    )(mem, w_mem_kv)


def _mem_attention(proj, kv, layer, *, batch, seq, n_mem):
    w = MEM_WIDTH
    tq = seq // MEM_TQ
    return pl.pallas_call(
        _mem_attn_kernel,
        grid=(batch, tq),
        in_specs=[pl.BlockSpec((MEM_TQ, w), lambda b, t: (b * tq + t, OFF_QM // w)),
                  pl.BlockSpec((n_mem, w), lambda b, t: (b, 2 * layer)),
                  pl.BlockSpec((n_mem, w), lambda b, t: (b, 2 * layer + 1)),
                  pl.BlockSpec((MEM_TQ, w), lambda b, t: (b * tq + t, OFF_GM // w))],
        out_specs=pl.BlockSpec((MEM_TQ, w), lambda b, t: (b * tq + t, 0)),
        out_shape=jax.ShapeDtypeStruct((batch * seq, MEM_WIDTH), BF16),
        compiler_params=_cparams(("arbitrary", "arbitrary"), 32),
        name="mem_attention",
    )(proj, kv, kv, proj)


def _out_proj_block(yd_ref, yc_ref, ym_ref, w_ref, zp_ref, mu_ref, rstd_ref, g_ref, b_ref, z_ref):
    c0, c1 = DIFF_WIDTH, DIFF_WIDTH + CONV_WIDTH
    acc = jnp.dot(yd_ref[...], w_ref[0:c0, :].astype(BF16), preferred_element_type=F32)
    acc = acc + jnp.dot(yc_ref[...], w_ref[c0:c1, :].astype(BF16), preferred_element_type=F32)
    acc = acc + jnp.dot(ym_ref[...], w_ref[c1:D_MODEL, :].astype(BF16), preferred_element_type=F32)
    tn = z_ref.shape[1]
    h = ((zp_ref[...] - _lane_tile(mu_ref[...], tn)) * _lane_tile(rstd_ref[...], tn) * g_ref[...]
         + b_ref[...])
    z_ref[...] = DEEPNORM_ALPHA * h + acc


def _out_proj_kernel(yd_hbm, yc_hbm, ym_hbm, w_hbm, zp_hbm, mu_hbm, rstd_hbm, g_hbm, b_hbm, z_hbm,
                     *, layer, tm, tn):
    m, n = z_hbm.shape
    piece = pl.BlockSpec((tm, tn), lambda j, i: (i, j))
    stat = pl.BlockSpec((tm, LANES), lambda j, i: (i, 0))
    vec = pl.BlockSpec((1, tn), lambda j, i: (0, j))
    pltpu.emit_pipeline(
        _out_proj_block,
        grid=(n // tn, m // tm),
        in_specs=[pl.BlockSpec((tm, DIFF_WIDTH), lambda j, i: (i, 0)),
                  pl.BlockSpec((tm, CONV_WIDTH), lambda j, i: (i, 0)),
                  pl.BlockSpec((tm, MEM_WIDTH), lambda j, i: (i, 0)),
                  pl.BlockSpec((D_MODEL, tn), lambda j, i: (0, j)),
                  piece, stat, stat, vec, vec],
        out_specs=[piece],
    )(yd_hbm, yc_hbm, ym_hbm, w_hbm.at[layer], zp_hbm, mu_hbm, rstd_hbm, g_hbm, b_hbm, z_hbm)


def _out_proj(yd, yc, ym, w_out, layer, z_prev, mu, rstd, g_prev, b_prev, *, tm=512, tn=1024):
    m = z_prev.shape[0]
    assert m % tm == 0 and D_MODEL % tn == 0
    return pl.pallas_call(
        functools.partial(_out_proj_kernel, layer=layer, tm=tm, tn=tn),
        in_specs=[pl.BlockSpec(memory_space=pl.ANY)] * 9,
        out_specs=pl.BlockSpec(memory_space=pl.ANY),
        out_shape=jax.ShapeDtypeStruct((m, D_MODEL), F32),
        compiler_params=pltpu.CompilerParams(vmem_limit_bytes=56 * MIB),
        name="out_proj",
    )(yd, yc, ym, w_out, z_prev, mu, rstd, g_prev.reshape(1, D_MODEL), b_prev.reshape(1, D_MODEL))


def kernel(x, mem, ln_in_g, ln_in_b, w_in, lambda_q1, lambda_k1, lambda_q2, lambda_k2, subln_g,
           conv_w, conv_b, conv_ln_g, conv_ln_b, w_conv_pw, w_mem_kv, w_out, ln_post_g, ln_post_b):
    batch, seq, d = x.shape
    n_mem = mem.shape[1]
    assert d == D_MODEL and seq % ATTN_TQ == 0 and seq % CONV_TS == 0 and seq % MEM_TQ == 0
    z = x.reshape(batch * seq, d)
    ln_g, ln_b = ln_in_g, ln_in_b
    kv = _mem_kv(mem.reshape(batch * n_mem, d), w_mem_kv)
    for layer in range(DEPTH):
        lam_init = 0.8 - 0.6 * math.exp(-0.3 * layer)
        h_bf16, mu, rstd = _layer_norm(z, ln_g, ln_b, final=False)
        proj = _in_proj(h_bf16, w_in, layer)
        yd = _diff_attention(proj, lambda_q1, lambda_k1, lambda_q2, lambda_k2, subln_g, layer,
                             batch=batch, seq=seq, lam_init=lam_init)
        yc = _conv_branch(proj, conv_w, conv_b, conv_ln_g, conv_ln_b, w_conv_pw, layer, seq=seq)
        ym = _mem_attention(proj, kv, layer, batch=batch, seq=seq, n_mem=n_mem)
        z = _out_proj(yd, yc, ym, w_out, layer, z, mu, rstd, ln_g, ln_b)
        ln_g, ln_b = ln_post_g[layer], ln_post_b[layer]
    return _layer_norm(z, ln_g, ln_b, final=True).reshape(batch, seq, d)
```

```python
import functools
import math

import jax
import jax.numpy as jnp
import numpy as np
from jax import lax
from jax.experimental import pallas as pl
from jax.experimental.pallas import tpu as pltpu

D_MODEL = 4096
DEPTH = 2
CHUNK = 64
DIFF_WIDTH = D_MODEL // 2
CONV_WIDTH = D_MODEL // 4
MEM_WIDTH = D_MODEL // 4
DIFF_HEADS = 8
DIFF_HEAD_DIM = DIFF_WIDTH // (2 * DIFF_HEADS)
DIFF_V_DIM = 2 * DIFF_HEAD_DIM
MEM_HEADS = 4
MEM_HEAD_DIM = MEM_WIDTH // MEM_HEADS
CONV_K = 31
LN_EPS = 1e-5
RMS_EPS = 1e-5
DEEPNORM_ALPHA = (2.0 * DEPTH) ** 0.25

OFF_Q = 0
OFF_K = OFF_Q + DIFF_WIDTH
OFF_V = OFF_K + DIFF_WIDTH
OFF_GD = OFF_V + DIFF_WIDTH
OFF_GLU_A = OFF_GD + DIFF_WIDTH
OFF_GLU_G = OFF_GLU_A + CONV_WIDTH
OFF_GC = OFF_GLU_G + CONV_WIDTH
OFF_QM = OFF_GC + CONV_WIDTH
OFF_GM = OFF_QM + MEM_WIDTH
IN_COLS = OFF_GM + MEM_WIDTH

LOG2E = 1.4426950408889634
Q_DIFF_SCALE = DIFF_HEAD_DIM ** -0.5 * LOG2E

LANES = 128
SUBLANES = 8
CONV_HALO = 32
ATTN_TQ = 256
MIB = 1024 * 1024

F32 = jnp.float32
BF16 = jnp.bfloat16


def _cparams(sem, vmem_mib):
    return pltpu.CompilerParams(dimension_semantics=sem, vmem_limit_bytes=int(vmem_mib * MIB))


def _silu(x):
    return x * jax.nn.sigmoid(x)


def _lane_tile(v, width):
    return jnp.concatenate([v] * (width // LANES), axis=1)


def _row_stats(x):
    mu = jnp.mean(x, axis=-1, keepdims=True)
    xc = x - mu
    var = jnp.mean(xc * xc, axis=-1, keepdims=True)
    return mu, xc, lax.rsqrt(var + LN_EPS)


def _ln_rows(x, g, b):
    _, xc, rstd = _row_stats(x)
    return xc * rstd * g + b


def _ln_stats_kernel(x_ref, g_ref, b_ref, hb_ref, mu_ref, rstd_ref):
    g, b = g_ref[...], b_ref[...]
    for r in range(0, x_ref.shape[0], SUBLANES):
        rows = slice(r, r + SUBLANES)
        mu, xc, rstd = _row_stats(x_ref[rows, :])
        hb_ref[rows, :] = (xc * rstd * g + b).astype(BF16)
        mu_ref[rows, :] = jnp.broadcast_to(mu, (SUBLANES, LANES))
        rstd_ref[rows, :] = jnp.broadcast_to(rstd, (SUBLANES, LANES))


def _ln_final_kernel(x_ref, g_ref, b_ref, o_ref):
    g, b = g_ref[...], b_ref[...]
    for r in range(0, x_ref.shape[0], SUBLANES):
        rows = slice(r, r + SUBLANES)
        o_ref[rows, :] = _ln_rows(x_ref[rows, :], g, b)


LN_INPUT_BUFFERS = 3


def _ln_pipeline_kernel(x_hbm, g_hbm, b_hbm, *out_hbm, final, tm):
    m, d = x_hbm.shape
    row = pl.BlockSpec((tm, d), lambda i: (i, 0))
    vec = pl.BlockSpec((1, d), lambda i: (0, 0))
    stat = pl.BlockSpec((tm, LANES), lambda i: (i, 0))
    pltpu.emit_pipeline(
        _ln_final_kernel if final else _ln_stats_kernel,
        grid=(m // tm,),
        in_specs=[pl.BlockSpec((tm, d), lambda i: (i, 0), pipeline_mode=pl.Buffered(LN_INPUT_BUFFERS)),
                  vec, vec],
        out_specs=[row] if final else [row, stat, stat],
    )(x_hbm, g_hbm, b_hbm, *out_hbm)


def _layer_norm(x, g, b, *, final, tm=512):
    m, d = x.shape
    assert m % tm == 0
    any_spec = pl.BlockSpec(memory_space=pl.ANY)
    if final:
        out_specs, out_shape = any_spec, jax.ShapeDtypeStruct((m, d), F32)
    else:
        out_specs = [any_spec] * 3
        out_shape = [jax.ShapeDtypeStruct((m, d), BF16), jax.ShapeDtypeStruct((m, LANES), F32),
                     jax.ShapeDtypeStruct((m, LANES), F32)]
    return pl.pallas_call(
        functools.partial(_ln_pipeline_kernel, final=final, tm=tm),
        in_specs=[any_spec] * 3, out_specs=out_specs, out_shape=out_shape,
        compiler_params=pltpu.CompilerParams(vmem_limit_bytes=56 * MIB),
        name="layer_norm_final" if final else "layer_norm",
    )(x, g.reshape(1, d), b.reshape(1, d))


IN_PROJ_SCALED = ((OFF_Q, OFF_K, Q_DIFF_SCALE), (OFF_QM, OFF_GM, MEM_HEAD_DIM ** -0.5))


def _matmul_kernel(a_ref, w_ref, o_ref):
    acc = jnp.dot(a_ref[...].astype(BF16), w_ref[...].astype(BF16), preferred_element_type=F32)
    o_ref[...] = acc.astype(o_ref.dtype)


def _in_proj_block(a_ref, w_ref, s_ref, o_ref):
    acc = jnp.dot(a_ref[...], w_ref[...].astype(BF16), preferred_element_type=F32)
    o_ref[...] = (acc * s_ref[...]).astype(o_ref.dtype)


def _in_proj_kernel(h_hbm, w_hbm, s_hbm, o_hbm, *, layer, tm, tn):
    m, k = h_hbm.shape
    n = o_hbm.shape[1]
    pltpu.emit_pipeline(
        _in_proj_block,
        grid=(m // tm, n // tn),
        in_specs=[pl.BlockSpec((tm, k), lambda i, j: (i, 0)),
                  pl.BlockSpec((k, tn), lambda i, j: (0, j)),
                  pl.BlockSpec((1, tn), lambda i, j: (0, j))],
        out_specs=[pl.BlockSpec((tm, tn), lambda i, j: (i, j))],
    )(h_hbm, w_hbm.at[layer], s_hbm, o_hbm)


def _in_proj(h, w_in, layer, *, tm=2048, tn=512):
    m, k = h.shape
    n = w_in.shape[-1]
    assert m % tm == 0 and n % tn == 0
    col_scale = np.ones((1, n), np.float32)
    for lo, hi, scale in IN_PROJ_SCALED:
        col_scale[0, lo:hi] = scale
    return pl.pallas_call(
        functools.partial(_in_proj_kernel, layer=layer, tm=tm, tn=tn),
        in_specs=[pl.BlockSpec(memory_space=pl.ANY)] * 3,
        out_specs=pl.BlockSpec(memory_space=pl.ANY),
        out_shape=jax.ShapeDtypeStruct((m, n), BF16),
        compiler_params=pltpu.CompilerParams(vmem_limit_bytes=58 * MIB),
        name="in_proj",
    )(h, w_in, jnp.asarray(col_scale))


_NT_DIMS = (((1,), (1,)), ((), ()))


def _diff_attn_kernel(q_ref, k_ref, v_ref, g_ref, lq1_ref, lk1_ref, lq2_ref, lk2_ref, sg_ref,
                      o_ref, s_ref, *, lam_init, seq):
    d = DIFF_HEAD_DIM
    lam = (jnp.exp(jnp.sum(lq1_ref[...] * lk1_ref[...], axis=-1, keepdims=True))
           - jnp.exp(jnp.sum(lq2_ref[...] * lk2_ref[...], axis=-1, keepdims=True))
           + lam_init)
    row_chunk = lax.broadcasted_iota(jnp.int32, (ATTN_TQ, ATTN_TQ), 0) // CHUNK
    col_chunk = lax.broadcasted_iota(jnp.int32, (ATTN_TQ, ATTN_TQ), 1) // CHUNK
    mask = col_chunk <= row_chunk
    out_gain = sg_ref[...] * (1.0 - lam_init)
    row_max = {}

    def scores(t):
        lo = t * ATTN_TQ
        for m in range(2):
            cols = slice(m * d, (m + 1) * d)
            q = q_ref[lo:lo + ATTN_TQ, cols]
            s_d = lax.dot_general(q, k_ref[lo:lo + ATTN_TQ, cols], _NT_DIMS, preferred_element_type=F32)
            s_d = jnp.where(mask, s_d, -jnp.inf)
            s_ref[t % 2, m, :, lo:lo + ATTN_TQ] = s_d
            mx = jnp.max(s_d, axis=-1, keepdims=True)
            if lo:
                s_o = lax.dot_general(q, k_ref[0:lo, cols], _NT_DIMS, preferred_element_type=F32)
                s_ref[t % 2, m, :, 0:lo] = s_o
                mx = jnp.maximum(mx, jnp.max(s_o, axis=-1, keepdims=True))
            row_max[(t, m)] = mx

    def outputs(t):
        lo = t * ATTN_TQ
        kv = lo + ATTN_TQ
        p = []
        for m in range(2):
            e = jnp.exp2(s_ref[t % 2, m, :, 0:kv] - row_max[(t, m)])
            l = jnp.sum(e, axis=-1, keepdims=True)
            p.append(e * ((lam if m else 1.0) / l))
        o = jnp.dot((p[0] - p[1]).astype(BF16), v_ref[0:kv, :], preferred_element_type=F32)
        o = o * lax.rsqrt(jnp.mean(o * o, axis=-1, keepdims=True) + RMS_EPS) * out_gain
        o_ref[lo:lo + ATTN_TQ, :] = (o * _silu(g_ref[lo:lo + ATTN_TQ, :].astype(F32))).astype(o_ref.dtype)

    n_tiles = seq // ATTN_TQ
    for t in range(n_tiles, -1, -1):
        if t >= 1:
            scores(t - 1)
        if t < n_tiles:
            outputs(t)


def _diff_attention(proj, lq1, lk1, lq2, lk2, subln_g, layer, *, batch, seq, lam_init):
    w = DIFF_V_DIM
    head_block = lambda off: pl.BlockSpec((seq, w), lambda b, h: (b, off // w + h))
    lam_spec = pl.BlockSpec((None, 1, DIFF_HEAD_DIM), lambda b, h: (layer, 0, 0))
    return pl.pallas_call(
        functools.partial(_diff_attn_kernel, lam_init=lam_init, seq=seq),
        grid=(batch, DIFF_HEADS),
        in_specs=[head_block(OFF_Q), head_block(OFF_K), head_block(OFF_V), head_block(OFF_GD),
                  lam_spec, lam_spec, lam_spec, lam_spec,
                  pl.BlockSpec((None, 1, DIFF_V_DIM), lambda b, h: (layer, 0, 0))],
        out_specs=pl.BlockSpec((seq, w), lambda b, h: (b, h)),
        out_shape=jax.ShapeDtypeStruct((batch * seq, DIFF_WIDTH), BF16),
        scratch_shapes=[pltpu.VMEM((2, 2, ATTN_TQ, seq), F32)],
        compiler_params=_cparams(("arbitrary", "arbitrary"), 48),
        name="diff_attention",
    )(proj, proj, proj, proj,
      lq1.reshape(DEPTH, 1, -1), lk1.reshape(DEPTH, 1, -1), lq2.reshape(DEPTH, 1, -1),
      lk2.reshape(DEPTH, 1, -1), subln_g.reshape(DEPTH, 1, -1))


CONV_TS = 512
CONV_RC = 64
CONV_PAD = CONV_HALO - (CONV_K - 1)


def _conv_kernel(a_ref, gt_ref, ah_ref, gh_ref, gc_ref, cw_ref, cb_ref, lg_ref, lb_ref, wpw_ref,
                 o_ref, hbuf_ref, shift_ref, cout_ref, wtap_ref, wpw_bf_ref, *, tiles_per_seq):
    i = pl.program_id(0)

    @pl.when(i == 0)
    def _():
        wpw_bf_ref[...] = wpw_ref[...].astype(BF16)
        for j in range(CONV_K):
            wtap_ref[j * SUBLANES:(j + 1) * SUBLANES, :] = jnp.broadcast_to(
                cw_ref[j:j + 1, :], (SUBLANES, CONV_WIDTH))

    halo = ah_ref[...].astype(F32) * jax.nn.sigmoid(gh_ref[...].astype(F32))
    keep = (i % tiles_per_seq != 0).astype(F32)
    hbuf_ref[0:CONV_HALO, :] = halo * keep
    hbuf_ref[CONV_HALO:CONV_HALO + CONV_TS, :] = (
        a_ref[...].astype(F32) * jax.nn.sigmoid(gt_ref[...].astype(F32)))

    n_shifted = CONV_HALO + CONV_TS - SUBLANES
    for k in range(1, SUBLANES):
        shift_ref[k - 1, 0:n_shifted, :] = hbuf_ref[k:k + n_shifted, :]
    n_sub = CONV_RC // SUBLANES
    for r in range(CONV_TS // CONV_RC):
        acc = [jnp.zeros((SUBLANES, CONV_WIDTH), F32) for _ in range(n_sub)]
        for j in range(CONV_K):
            k, base = (CONV_PAD + j) % SUBLANES, (CONV_PAD + j) // SUBLANES * SUBLANES
            wj = wtap_ref[j * SUBLANES:(j + 1) * SUBLANES, :]
            for s in range(n_sub):
                row = base + r * CONV_RC + s * SUBLANES
                win = shift_ref[k - 1, row:row + SUBLANES, :] if k else hbuf_ref[row:row + SUBLANES, :]
                acc[s] = acc[s] + wj * win
        for s in range(n_sub):
            row = r * CONV_RC + s * SUBLANES
            cout_ref[row:row + SUBLANES, :] = acc[s]

    c = cout_ref[...] + cb_ref[...]
    c = _silu(_ln_rows(c, lg_ref[...], lb_ref[...]))
    y = jnp.dot(c.astype(BF16), wpw_bf_ref[...], preferred_element_type=F32)
    o_ref[...] = (y * _silu(gc_ref[...].astype(F32))).astype(o_ref.dtype)


def _conv_branch(proj, conv_w, conv_b, ln_g, ln_b, w_pw, layer, *, seq):
    m = proj.shape[0]
    cw = CONV_WIDTH
    halo_per_tile = CONV_TS // CONV_HALO
    main = lambda off: pl.BlockSpec((CONV_TS, cw), lambda i: (i, off // cw))
    halo = lambda off: pl.BlockSpec(
        (CONV_HALO, cw), lambda i: (jnp.maximum(i * halo_per_tile - 1, 0), off // cw))
    vec = pl.BlockSpec((None, 1, cw), lambda i: (layer, 0, 0))
    return pl.pallas_call(
        functools.partial(_conv_kernel, tiles_per_seq=seq // CONV_TS),
        grid=(m // CONV_TS,),
        in_specs=[main(OFF_GLU_A), main(OFF_GLU_G), halo(OFF_GLU_A), halo(OFF_GLU_G), main(OFF_GC),
                  pl.BlockSpec((None, CONV_K, cw), lambda i: (layer, 0, 0)),
                  vec, vec, vec,
                  pl.BlockSpec((None, cw, cw), lambda i: (layer, 0, 0))],
        out_specs=pl.BlockSpec((CONV_TS, cw), lambda i: (i, 0)),
        out_shape=jax.ShapeDtypeStruct((m, cw), BF16),
        scratch_shapes=[pltpu.VMEM((CONV_HALO + CONV_TS, cw), F32),
                        pltpu.VMEM((SUBLANES - 1, CONV_HALO + CONV_TS, cw), F32),
                        pltpu.VMEM((CONV_TS, cw), F32),
                        pltpu.VMEM((CONV_K * SUBLANES, cw), F32),
                        pltpu.VMEM((cw, cw), BF16)],
        compiler_params=_cparams(("arbitrary",), 48),
        name="conv_branch",
    )(proj, proj, proj, proj, proj, conv_w, conv_b.reshape(DEPTH, 1, cw),
      ln_g.reshape(DEPTH, 1, cw), ln_b.reshape(DEPTH, 1, cw), w_pw)


MEM_TQ = 2048


def _mem_attn_kernel(q_ref, k_ref, v_ref, g_ref, o_ref):
    for h in range(MEM_HEADS):
        cols = slice(h * MEM_HEAD_DIM, (h + 1) * MEM_HEAD_DIM)
        s = lax.dot_general(q_ref[:, cols], k_ref[:, cols], _NT_DIMS, preferred_element_type=F32)
        p = jnp.exp(s - jnp.max(s, axis=-1, keepdims=True))
        l = jnp.sum(p, axis=-1, keepdims=True)
        o = jnp.dot(p.astype(BF16), v_ref[:, cols], preferred_element_type=F32) * (1.0 / l)
        o_ref[:, cols] = (o * _silu(g_ref[:, cols].astype(F32))).astype(o_ref.dtype)


def _mem_kv(mem, w_mem_kv, *, tn=512):
    m, k = mem.shape
    n = w_mem_kv.shape[-1]
    per_layer = n // tn
    return pl.pallas_call(
        _matmul_kernel,
        grid=(DEPTH * per_layer,),
        in_specs=[pl.BlockSpec((m, k), lambda j: (0, 0), pipeline_mode=pl.Buffered(1)),
                  pl.BlockSpec((None, k, tn), lambda j: (j // per_layer, 0, j % per_layer))],
        out_specs=pl.BlockSpec((m, tn), lambda j: (0, j)),
        out_shape=jax.ShapeDtypeStruct((m, DEPTH * n), BF16),
        compiler_params=_cparams(("arbitrary",), 52),
        name="mem_kv",
    )(mem, w_mem_kv)


def _mem_attention(proj, kv, layer, *, batch, seq, n_mem):
    w = MEM_WIDTH
    tq = seq // MEM_TQ
    return pl.pallas_call(
        _mem_attn_kernel,
        grid=(batch, tq),
        in_specs=[pl.BlockSpec((MEM_TQ, w), lambda b, t: (b * tq + t, OFF_QM // w)),
                  pl.BlockSpec((n_mem, w), lambda b, t: (b, 2 * layer)),
                  pl.BlockSpec((n_mem, w), lambda b, t: (b, 2 * layer + 1)),
                  pl.BlockSpec((MEM_TQ, w), lambda b, t: (b * tq + t, OFF_GM // w))],
        out_specs=pl.BlockSpec((MEM_TQ, w), lambda b, t: (b * tq + t, 0)),
        out_shape=jax.ShapeDtypeStruct((batch * seq, MEM_WIDTH), BF16),
        compiler_params=_cparams(("arbitrary", "arbitrary"), 32),
        name="mem_attention",
    )(proj, kv, kv, proj)


def _out_proj_block(yd_ref, yc_ref, ym_ref, w_ref, zp_ref, mu_ref, rstd_ref, g_ref, b_ref, z_ref):
    c0, c1 = DIFF_WIDTH, DIFF_WIDTH + CONV_WIDTH
    acc = jnp.dot(yd_ref[...], w_ref[0:c0, :].astype(BF16), preferred_element_type=F32)
    acc = acc + jnp.dot(yc_ref[...], w_ref[c0:c1, :].astype(BF16), preferred_element_type=F32)
    acc = acc + jnp.dot(ym_ref[...], w_ref[c1:D_MODEL, :].astype(BF16), preferred_element_type=F32)
    tn = z_ref.shape[1]
    h = ((zp_ref[...] - _lane_tile(mu_ref[...], tn)) * _lane_tile(rstd_ref[...], tn) * g_ref[...]
         + b_ref[...])
    z_ref[...] = DEEPNORM_ALPHA * h + acc


def _out_proj_kernel(yd_hbm, yc_hbm, ym_hbm, w_hbm, zp_hbm, mu_hbm, rstd_hbm, g_hbm, b_hbm, z_hbm,
                     *, layer, tm, tn):
    m, n = z_hbm.shape
    piece = pl.BlockSpec((tm, tn), lambda j, i: (i, j))
    stat = pl.BlockSpec((tm, LANES), lambda j, i: (i, 0))
    vec = pl.BlockSpec((1, tn), lambda j, i: (0, j))
    pltpu.emit_pipeline(
        _out_proj_block,
        grid=(n // tn, m // tm),
        in_specs=[pl.BlockSpec((tm, DIFF_WIDTH), lambda j, i: (i, 0)),
                  pl.BlockSpec((tm, CONV_WIDTH), lambda j, i: (i, 0)),
                  pl.BlockSpec((tm, MEM_WIDTH), lambda j, i: (i, 0)),
                  pl.BlockSpec((D_MODEL, tn), lambda j, i: (0, j)),
                  piece, stat, stat, vec, vec],
        out_specs=[piece],
    )(yd_hbm, yc_hbm, ym_hbm, w_hbm.at[layer], zp_hbm, mu_hbm, rstd_hbm, g_hbm, b_hbm, z_hbm)


def _out_proj(yd, yc, ym, w_out, layer, z_prev, mu, rstd, g_prev, b_prev, *, tm=512, tn=1024):
    m = z_prev.shape[0]
    assert m % tm == 0 and D_MODEL % tn == 0
    return pl.pallas_call(
        functools.partial(_out_proj_kernel, layer=layer, tm=tm, tn=tn),
        in_specs=[pl.BlockSpec(memory_space=pl.ANY)] * 9,
        out_specs=pl.BlockSpec(memory_space=pl.ANY),
        out_shape=jax.ShapeDtypeStruct((m, D_MODEL), F32),
        compiler_params=pltpu.CompilerParams(vmem_limit_bytes=56 * MIB),
        name="out_proj",
    )(yd, yc, ym, w_out, z_prev, mu, rstd, g_prev.reshape(1, D_MODEL), b_prev.reshape(1, D_MODEL))


def kernel(x, mem, ln_in_g, ln_in_b, w_in, lambda_q1, lambda_k1, lambda_q2, lambda_k2, subln_g,
           conv_w, conv_b, conv_ln_g, conv_ln_b, w_conv_pw, w_mem_kv, w_out, ln_post_g, ln_post_b):
    batch, seq, d = x.shape
    n_mem = mem.shape[1]
    assert d == D_MODEL and seq % ATTN_TQ == 0 and seq % CONV_TS == 0 and seq % MEM_TQ == 0
    z = x.reshape(batch * seq, d)
    ln_g, ln_b = ln_in_g, ln_in_b
    kv = _mem_kv(mem.reshape(batch * n_mem, d), w_mem_kv)
    for layer in range(DEPTH):
        lam_init = 0.8 - 0.6 * math.exp(-0.3 * layer)
        h_bf16, mu, rstd = _layer_norm(z, ln_g, ln_b, final=False)
        proj = _in_proj(h_bf16, w_in, layer)
        yd = _diff_attention(proj, lambda_q1, lambda_k1, lambda_q2, lambda_k2, subln_g, layer,
                             batch=batch, seq=seq, lam_init=lam_init)
        yc = _conv_branch(proj, conv_w, conv_b, conv_ln_g, conv_ln_b, w_conv_pw, layer, seq=seq)
        ym = _mem_attention(proj, kv, layer, batch=batch, seq=seq, n_mem=n_mem)
        z = _out_proj(yd, yc, ym, w_out, layer, z, mu, rstd, ln_g, ln_b)
        ln_g, ln_b = ln_post_g[layer], ln_post_b[layer]
    return _layer_norm(z, ln_g, ln_b, final=True).reshape(batch, seq, d)
```

```python
import functools
import math

import jax
import jax.numpy as jnp
from jax import lax
from jax.experimental import pallas as pl
from jax.experimental.pallas import tpu as pltpu

D_MODEL = 4096
DEPTH = 2
CHUNK = 64
DIFF_WIDTH = D_MODEL // 2
CONV_WIDTH = D_MODEL // 4
MEM_WIDTH = D_MODEL // 4
DIFF_HEADS = 8
DIFF_HEAD_DIM = DIFF_WIDTH // (2 * DIFF_HEADS)
DIFF_V_DIM = 2 * DIFF_HEAD_DIM
MEM_HEADS = 4
MEM_HEAD_DIM = MEM_WIDTH // MEM_HEADS
CONV_K = 31
LN_EPS = 1e-5
RMS_EPS = 1e-5
DEEPNORM_ALPHA = (2.0 * DEPTH) ** 0.25

OFF_Q = 0
OFF_K = OFF_Q + DIFF_WIDTH
OFF_V = OFF_K + DIFF_WIDTH
OFF_GD = OFF_V + DIFF_WIDTH
OFF_GLU_A = OFF_GD + DIFF_WIDTH
OFF_GLU_G = OFF_GLU_A + CONV_WIDTH
OFF_GC = OFF_GLU_G + CONV_WIDTH
OFF_QM = OFF_GC + CONV_WIDTH
OFF_GM = OFF_QM + MEM_WIDTH
IN_COLS = OFF_GM + MEM_WIDTH

LOG2E = 1.4426950408889634
Q_DIFF_SCALE = DIFF_HEAD_DIM ** -0.5 * LOG2E

LANES = 128
SUBLANES = 8
CONV_HALO = 32
ATTN_TQ = 256
MIB = 1024 * 1024

F32 = jnp.float32
BF16 = jnp.bfloat16


def _cparams(sem, vmem_mib):
    return pltpu.CompilerParams(dimension_semantics=sem, vmem_limit_bytes=int(vmem_mib * MIB))


def _silu(x):
    return x * jax.nn.sigmoid(x)


def _lane_tile(v, width):
    return jnp.concatenate([v] * (width // LANES), axis=1)


def _row_stats(x):
    mu = jnp.mean(x, axis=-1, keepdims=True)
    xc = x - mu
    var = jnp.mean(xc * xc, axis=-1, keepdims=True)
    return mu, xc, lax.rsqrt(var + LN_EPS)


def _ln_rows(x, g, b):
    _, xc, rstd = _row_stats(x)
    return xc * rstd * g + b


def _ln_stats_kernel(x_ref, g_ref, b_ref, hb_ref, mu_ref, rstd_ref):
    g, b = g_ref[...], b_ref[...]
    for r in range(0, x_ref.shape[0], SUBLANES):
        rows = slice(r, r + SUBLANES)
        mu, xc, rstd = _row_stats(x_ref[rows, :])
        hb_ref[rows, :] = (xc * rstd * g + b).astype(BF16)
        mu_ref[rows, :] = jnp.broadcast_to(mu, (SUBLANES, LANES))
        rstd_ref[rows, :] = jnp.broadcast_to(rstd, (SUBLANES, LANES))


def _ln_final_kernel(x_ref, g_ref, b_ref, o_ref):
    g, b = g_ref[...], b_ref[...]
    for r in range(0, x_ref.shape[0], SUBLANES):
        rows = slice(r, r + SUBLANES)
        o_ref[rows, :] = _ln_rows(x_ref[rows, :], g, b)


LN_INPUT_BUFFERS = 3


def _ln_stats_pipeline(x_hbm, g_hbm, b_hbm, hb_hbm, mu_hbm, rstd_hbm, *, tm):
    m, d = x_hbm.shape
    row = pl.BlockSpec((tm, d), lambda i: (i, 0))
    vec = pl.BlockSpec((1, d), lambda i: (0, 0))
    stat = pl.BlockSpec((tm, LANES), lambda i: (i, 0))
    pltpu.emit_pipeline(
        _ln_stats_kernel,
        grid=(m // tm,),
        in_specs=[pl.BlockSpec((tm, d), lambda i: (i, 0), pipeline_mode=pl.Buffered(LN_INPUT_BUFFERS)),
                  vec, vec],
        out_specs=[row, stat, stat],
    )(x_hbm, g_hbm, b_hbm, hb_hbm, mu_hbm, rstd_hbm)


def _layer_norm(x, g, b, *, final, tm=512):
    m, d = x.shape
    assert m % tm == 0
    if not final:
        any_spec = pl.BlockSpec(memory_space=pl.ANY)
        return pl.pallas_call(
            functools.partial(_ln_stats_pipeline, tm=tm),
            in_specs=[any_spec] * 3, out_specs=[any_spec] * 3,
            out_shape=[jax.ShapeDtypeStruct((m, d), BF16), jax.ShapeDtypeStruct((m, LANES), F32),
                       jax.ShapeDtypeStruct((m, LANES), F32)],
            compiler_params=pltpu.CompilerParams(vmem_limit_bytes=56 * MIB),
            name="layer_norm",
        )(x, g.reshape(1, d), b.reshape(1, d))
    row = pl.BlockSpec((tm, d), lambda i: (i, 0))
    vec = pl.BlockSpec((1, d), lambda i: (0, 0))
    return pl.pallas_call(
        _ln_final_kernel, grid=(m // tm,), in_specs=[row, vec, vec], out_specs=row,
        out_shape=jax.ShapeDtypeStruct((m, d), F32),
        compiler_params=_cparams(("arbitrary",), 48),
        name="layer_norm_final",
    )(x, g.reshape(1, d), b.reshape(1, d))


IN_PROJ_SCALED = ((OFF_Q, OFF_K, Q_DIFF_SCALE), (OFF_QM, OFF_GM, MEM_HEAD_DIM ** -0.5))


def _matmul_kernel(a_ref, w_ref, o_ref, *, col_axis, col_scales):
    acc = jnp.dot(a_ref[...].astype(BF16), w_ref[...].astype(BF16), preferred_element_type=F32)
    if col_scales:
        col = pl.program_id(col_axis) * o_ref.shape[1]
        scale = jnp.float32(1.0)
        for lo, hi, s in col_scales:
            scale = jnp.where((col >= lo) & (col < hi), jnp.float32(s), scale)
        acc = acc * scale
    o_ref[...] = acc.astype(o_ref.dtype)


def _in_proj(h, w_in, layer, *, tm=2048, tn=512):
    m, k = h.shape
    n = w_in.shape[-1]
    assert all(lo % tn == 0 and hi % tn == 0 for lo, hi, _ in IN_PROJ_SCALED)
    return pl.pallas_call(
        functools.partial(_matmul_kernel, col_axis=1, col_scales=IN_PROJ_SCALED),
        grid=(m // tm, n // tn),
        in_specs=[pl.BlockSpec((tm, k), lambda i, j: (i, 0)),
                  pl.BlockSpec((None, k, tn), lambda i, j: (layer, 0, j))],
        out_specs=pl.BlockSpec((tm, tn), lambda i, j: (i, j)),
        out_shape=jax.ShapeDtypeStruct((m, n), BF16),
        compiler_params=_cparams(("arbitrary", "arbitrary"), 58),
        name="in_proj",
    )(h, w_in)


_NT_DIMS = (((1,), (1,)), ((), ()))


def _diff_attn_kernel(q_ref, k_ref, v_ref, g_ref, lq1_ref, lk1_ref, lq2_ref, lk2_ref, sg_ref,
                      o_ref, s_ref, *, lam_init, seq):
    d = DIFF_HEAD_DIM
    lam = (jnp.exp(jnp.sum(lq1_ref[...] * lk1_ref[...], axis=-1, keepdims=True))
           - jnp.exp(jnp.sum(lq2_ref[...] * lk2_ref[...], axis=-1, keepdims=True))
           + lam_init)
    row_chunk = lax.broadcasted_iota(jnp.int32, (ATTN_TQ, ATTN_TQ), 0) // CHUNK
    col_chunk = lax.broadcasted_iota(jnp.int32, (ATTN_TQ, ATTN_TQ), 1) // CHUNK
    mask = col_chunk <= row_chunk
    out_gain = sg_ref[...] * (1.0 - lam_init)
    row_max = {}

    def scores(t):
        lo = t * ATTN_TQ
        for m in range(2):
            cols = slice(m * d, (m + 1) * d)
            q = q_ref[lo:lo + ATTN_TQ, cols]
            s_d = lax.dot_general(q, k_ref[lo:lo + ATTN_TQ, cols], _NT_DIMS, preferred_element_type=F32)
            s_d = jnp.where(mask, s_d, -jnp.inf)
            s_ref[t % 2, m, :, lo:lo + ATTN_TQ] = s_d
            mx = jnp.max(s_d, axis=-1, keepdims=True)
            if lo:
                s_o = lax.dot_general(q, k_ref[0:lo, cols], _NT_DIMS, preferred_element_type=F32)
                s_ref[t % 2, m, :, 0:lo] = s_o
                mx = jnp.maximum(mx, jnp.max(s_o, axis=-1, keepdims=True))
            row_max[(t, m)] = mx

    def outputs(t):
        lo = t * ATTN_TQ
        kv = lo + ATTN_TQ
        p = []
        for m in range(2):
            e = jnp.exp2(s_ref[t % 2, m, :, 0:kv] - row_max[(t, m)])
            l = jnp.sum(e, axis=-1, keepdims=True)
            p.append(e * ((lam if m else 1.0) / l))
        o = jnp.dot((p[0] - p[1]).astype(BF16), v_ref[0:kv, :], preferred_element_type=F32)
        o = o * lax.rsqrt(jnp.mean(o * o, axis=-1, keepdims=True) + RMS_EPS) * out_gain
        o_ref[lo:lo + ATTN_TQ, :] = (o * _silu(g_ref[lo:lo + ATTN_TQ, :].astype(F32))).astype(o_ref.dtype)

    n_tiles = seq // ATTN_TQ
    for t in range(n_tiles, -1, -1):
        if t >= 1:
            scores(t - 1)
        if t < n_tiles:
            outputs(t)


def _diff_attention(proj, lq1, lk1, lq2, lk2, subln_g, layer, *, batch, seq, lam_init):
    w = DIFF_V_DIM
    head_block = lambda off: pl.BlockSpec((seq, w), lambda b, h: (b, off // w + h))
    lam_spec = pl.BlockSpec((None, 1, DIFF_HEAD_DIM), lambda b, h: (layer, 0, 0))
    return pl.pallas_call(
        functools.partial(_diff_attn_kernel, lam_init=lam_init, seq=seq),
        grid=(batch, DIFF_HEADS),
        in_specs=[head_block(OFF_Q), head_block(OFF_K), head_block(OFF_V), head_block(OFF_GD),
                  lam_spec, lam_spec, lam_spec, lam_spec,
                  pl.BlockSpec((None, 1, DIFF_V_DIM), lambda b, h: (layer, 0, 0))],
        out_specs=pl.BlockSpec((seq, w), lambda b, h: (b, h)),
        out_shape=jax.ShapeDtypeStruct((batch * seq, DIFF_WIDTH), BF16),
        scratch_shapes=[pltpu.VMEM((2, 2, ATTN_TQ, seq), F32)],
        compiler_params=_cparams(("arbitrary", "arbitrary"), 48),
        name="diff_attention",
    )(proj, proj, proj, proj,
      lq1.reshape(DEPTH, 1, -1), lk1.reshape(DEPTH, 1, -1), lq2.reshape(DEPTH, 1, -1),
      lk2.reshape(DEPTH, 1, -1), subln_g.reshape(DEPTH, 1, -1))


CONV_TS = 512
CONV_RC = 64
CONV_PAD = CONV_HALO - (CONV_K - 1)


def _conv_kernel(a_ref, gt_ref, ah_ref, gh_ref, gc_ref, cw_ref, cb_ref, lg_ref, lb_ref, wpw_ref,
                 o_ref, hbuf_ref, shift_ref, cout_ref, wtap_ref, wpw_bf_ref, *, tiles_per_seq):
    i = pl.program_id(0)

    @pl.when(i == 0)
    def _():
        wpw_bf_ref[...] = wpw_ref[...].astype(BF16)
        for j in range(CONV_K):
            wtap_ref[j * SUBLANES:(j + 1) * SUBLANES, :] = jnp.broadcast_to(
                cw_ref[j:j + 1, :], (SUBLANES, CONV_WIDTH))

    halo = ah_ref[...].astype(F32) * jax.nn.sigmoid(gh_ref[...].astype(F32))
    keep = (i % tiles_per_seq != 0).astype(F32)
    hbuf_ref[0:CONV_HALO, :] = halo * keep
    hbuf_ref[CONV_HALO:CONV_HALO + CONV_TS, :] = (
        a_ref[...].astype(F32) * jax.nn.sigmoid(gt_ref[...].astype(F32)))

    n_shifted = CONV_HALO + CONV_TS - SUBLANES
    for k in range(1, SUBLANES):
        shift_ref[k - 1, 0:n_shifted, :] = hbuf_ref[k:k + n_shifted, :]
    n_sub = CONV_RC // SUBLANES
    for r in range(CONV_TS // CONV_RC):
        acc = [jnp.zeros((SUBLANES, CONV_WIDTH), F32) for _ in range(n_sub)]
        for j in range(CONV_K):
            k, base = (CONV_PAD + j) % SUBLANES, (CONV_PAD + j) // SUBLANES * SUBLANES
            wj = wtap_ref[j * SUBLANES:(j + 1) * SUBLANES, :]
            for s in range(n_sub):
                row = base + r * CONV_RC + s * SUBLANES
                win = shift_ref[k - 1, row:row + SUBLANES, :] if k else hbuf_ref[row:row + SUBLANES, :]
                acc[s] = acc[s] + wj * win
        for s in range(n_sub):
            row = r * CONV_RC + s * SUBLANES
            cout_ref[row:row + SUBLANES, :] = acc[s]

    c = cout_ref[...] + cb_ref[...]
    c = _silu(_ln_rows(c, lg_ref[...], lb_ref[...]))
    y = jnp.dot(c.astype(BF16), wpw_bf_ref[...], preferred_element_type=F32)
    o_ref[...] = (y * _silu(gc_ref[...].astype(F32))).astype(o_ref.dtype)


def _conv_branch(proj, conv_w, conv_b, ln_g, ln_b, w_pw, layer, *, seq):
    m = proj.shape[0]
    cw = CONV_WIDTH
    halo_per_tile = CONV_TS // CONV_HALO
    main = lambda off: pl.BlockSpec((CONV_TS, cw), lambda i: (i, off // cw))
    halo = lambda off: pl.BlockSpec(
        (CONV_HALO, cw), lambda i: (jnp.maximum(i * halo_per_tile - 1, 0), off // cw))
    vec = pl.BlockSpec((None, 1, cw), lambda i: (layer, 0, 0))
    return pl.pallas_call(
        functools.partial(_conv_kernel, tiles_per_seq=seq // CONV_TS),
        grid=(m // CONV_TS,),
        in_specs=[main(OFF_GLU_A), main(OFF_GLU_G), halo(OFF_GLU_A), halo(OFF_GLU_G), main(OFF_GC),
                  pl.BlockSpec((None, CONV_K, cw), lambda i: (layer, 0, 0)),
                  vec, vec, vec,
                  pl.BlockSpec((None, cw, cw), lambda i: (layer, 0, 0))],
        out_specs=pl.BlockSpec((CONV_TS, cw), lambda i: (i, 0)),
        out_shape=jax.ShapeDtypeStruct((m, cw), BF16),
        scratch_shapes=[pltpu.VMEM((CONV_HALO + CONV_TS, cw), F32),
                        pltpu.VMEM((SUBLANES - 1, CONV_HALO + CONV_TS, cw), F32),
                        pltpu.VMEM((CONV_TS, cw), F32),
                        pltpu.VMEM((CONV_K * SUBLANES, cw), F32),
                        pltpu.VMEM((cw, cw), BF16)],
        compiler_params=_cparams(("arbitrary",), 48),
        name="conv_branch",
    )(proj, proj, proj, proj, proj, conv_w, conv_b.reshape(DEPTH, 1, cw),
      ln_g.reshape(DEPTH, 1, cw), ln_b.reshape(DEPTH, 1, cw), w_pw)


MEM_TQ = 2048


def _mem_attn_kernel(q_ref, k_ref, v_ref, g_ref, o_ref):
    for h in range(MEM_HEADS):
        cols = slice(h * MEM_HEAD_DIM, (h + 1) * MEM_HEAD_DIM)
        s = lax.dot_general(q_ref[:, cols], k_ref[:, cols], _NT_DIMS, preferred_element_type=F32)
        p = jnp.exp(s - jnp.max(s, axis=-1, keepdims=True))
        l = jnp.sum(p, axis=-1, keepdims=True)
        o = jnp.dot(p.astype(BF16), v_ref[:, cols], preferred_element_type=F32) * (1.0 / l)
        o_ref[:, cols] = (o * _silu(g_ref[:, cols].astype(F32))).astype(o_ref.dtype)


def _mem_kv(mem, w_mem_kv, *, tn=512):
    m, k = mem.shape
    n = w_mem_kv.shape[-1]
    per_layer = n // tn
    return pl.pallas_call(
        functools.partial(_matmul_kernel, col_axis=0, col_scales=()),
        grid=(DEPTH * per_layer,),
        in_specs=[pl.BlockSpec((m, k), lambda j: (0, 0), pipeline_mode=pl.Buffered(1)),
                  pl.BlockSpec((None, k, tn), lambda j: (j // per_layer, 0, j % per_layer))],
        out_specs=pl.BlockSpec((m, tn), lambda j: (0, j)),
        out_shape=jax.ShapeDtypeStruct((m, DEPTH * n), BF16),
        compiler_params=_cparams(("arbitrary",), 52),
        name="mem_kv",
    )(mem, w_mem_kv)


def _mem_attention(proj, kv, layer, *, batch, seq, n_mem):
    w = MEM_WIDTH
    tq = seq // MEM_TQ
    return pl.pallas_call(
        _mem_attn_kernel,
        grid=(batch, tq),
        in_specs=[pl.BlockSpec((MEM_TQ, w), lambda b, t: (b * tq + t, OFF_QM // w)),
                  pl.BlockSpec((n_mem, w), lambda b, t: (b, 2 * layer)),
                  pl.BlockSpec((n_mem, w), lambda b, t: (b, 2 * layer + 1)),
                  pl.BlockSpec((MEM_TQ, w), lambda b, t: (b * tq + t, OFF_GM // w))],
        out_specs=pl.BlockSpec((MEM_TQ, w), lambda b, t: (b * tq + t, 0)),
        out_shape=jax.ShapeDtypeStruct((batch * seq, MEM_WIDTH), BF16),
        compiler_params=_cparams(("arbitrary", "arbitrary"), 32),
        name="mem_attention",
    )(proj, kv, kv, proj)


def _out_proj_kernel(yd_ref, yc_ref, ym_ref, w_ref, zp_ref, mu_ref, rstd_ref, g_ref, b_ref, z_ref):
    c0, c1 = DIFF_WIDTH, DIFF_WIDTH + CONV_WIDTH
    acc = jnp.dot(yd_ref[...], w_ref[0:c0, :].astype(BF16), preferred_element_type=F32)
    acc = acc + jnp.dot(yc_ref[...], w_ref[c0:c1, :].astype(BF16), preferred_element_type=F32)
    acc = acc + jnp.dot(ym_ref[...], w_ref[c1:D_MODEL, :].astype(BF16), preferred_element_type=F32)
    tn = z_ref.shape[1]
    h = ((zp_ref[...] - _lane_tile(mu_ref[...], tn)) * _lane_tile(rstd_ref[...], tn) * g_ref[...]
         + b_ref[...])
    z_ref[...] = DEEPNORM_ALPHA * h + acc


def _out_proj(yd, yc, ym, w_out, layer, z_prev, mu, rstd, g_prev, b_prev, *, tm=512, tn=1024):
    m = z_prev.shape[0]
    piece = pl.BlockSpec((tm, tn), lambda j, i: (i, j))
    stat = pl.BlockSpec((tm, LANES), lambda j, i: (i, 0))
    vec = pl.BlockSpec((1, tn), lambda j, i: (0, j))
    return pl.pallas_call(
        _out_proj_kernel,
        grid=(D_MODEL // tn, m // tm),
        in_specs=[pl.BlockSpec((tm, DIFF_WIDTH), lambda j, i: (i, 0)),
                  pl.BlockSpec((tm, CONV_WIDTH), lambda j, i: (i, 0)),
                  pl.BlockSpec((tm, MEM_WIDTH), lambda j, i: (i, 0)),
                  pl.BlockSpec((None, D_MODEL, tn), lambda j, i: (layer, 0, j)),
                  piece, stat, stat, vec, vec],
        out_specs=piece,
        out_shape=jax.ShapeDtypeStruct((m, D_MODEL), F32),
        compiler_params=_cparams(("arbitrary", "arbitrary"), 56),
        name="out_proj",
    )(yd, yc, ym, w_out, z_prev, mu, rstd, g_prev.reshape(1, D_MODEL), b_prev.reshape(1, D_MODEL))


def kernel(x, mem, ln_in_g, ln_in_b, w_in, lambda_q1, lambda_k1, lambda_q2, lambda_k2, subln_g,
           conv_w, conv_b, conv_ln_g, conv_ln_b, w_conv_pw, w_mem_kv, w_out, ln_post_g, ln_post_b):
    batch, seq, d = x.shape
    n_mem = mem.shape[1]
    assert d == D_MODEL and seq % ATTN_TQ == 0 and seq % CONV_TS == 0 and seq % MEM_TQ == 0
    z = x.reshape(batch * seq, d)
    ln_g, ln_b = ln_in_g, ln_in_b
    kv = _mem_kv(mem.reshape(batch * n_mem, d), w_mem_kv)
    for layer in range(DEPTH):
        lam_init = 0.8 - 0.6 * math.exp(-0.3 * layer)
        h_bf16, mu, rstd = _layer_norm(z, ln_g, ln_b, final=False)
        proj = _in_proj(h_bf16, w_in, layer)
        yd = _diff_attention(proj, lambda_q1, lambda_k1, lambda_q2, lambda_k2, subln_g, layer,
                             batch=batch, seq=seq, lam_init=lam_init)
        yc = _conv_branch(proj, conv_w, conv_b, conv_ln_g, conv_ln_b, w_conv_pw, layer, seq=seq)
        ym = _mem_attention(proj, kv, layer, batch=batch, seq=seq, n_mem=n_mem)
        z = _out_proj(yd, yc, ym, w_out, layer, z, mu, rstd, ln_g, ln_b)
        ln_g, ln_b = ln_post_g[layer], ln_post_b[layer]
    return _layer_norm(z, ln_g, ln_b, final=True).reshape(batch, seq, d)
```

```python
import functools
import math

import jax
import jax.numpy as jnp
from jax import lax
from jax.experimental import pallas as pl
from jax.experimental.pallas import tpu as pltpu

D_MODEL = 4096
DEPTH = 2
CHUNK = 64
DIFF_WIDTH = D_MODEL // 2
CONV_WIDTH = D_MODEL // 4
MEM_WIDTH = D_MODEL // 4
DIFF_HEADS = 8
DIFF_HEAD_DIM = DIFF_WIDTH // (2 * DIFF_HEADS)
DIFF_V_DIM = 2 * DIFF_HEAD_DIM
MEM_HEADS = 4
MEM_HEAD_DIM = MEM_WIDTH // MEM_HEADS
CONV_K = 31
LN_EPS = 1e-5
RMS_EPS = 1e-5
DEEPNORM_ALPHA = (2.0 * DEPTH) ** 0.25

OFF_Q = 0
OFF_K = OFF_Q + DIFF_WIDTH
OFF_V = OFF_K + DIFF_WIDTH
OFF_GD = OFF_V + DIFF_WIDTH
OFF_GLU_A = OFF_GD + DIFF_WIDTH
OFF_GLU_G = OFF_GLU_A + CONV_WIDTH
OFF_GC = OFF_GLU_G + CONV_WIDTH
OFF_QM = OFF_GC + CONV_WIDTH
OFF_GM = OFF_QM + MEM_WIDTH
IN_COLS = OFF_GM + MEM_WIDTH

LOG2E = 1.4426950408889634
Q_DIFF_SCALE = DIFF_HEAD_DIM ** -0.5 * LOG2E

LANES = 128
SUBLANES = 8
CONV_HALO = 32
ATTN_TQ = 256
MIB = 1024 * 1024

F32 = jnp.float32
BF16 = jnp.bfloat16


def _cparams(sem, vmem_mib):
    return pltpu.CompilerParams(dimension_semantics=sem, vmem_limit_bytes=int(vmem_mib * MIB))


def _silu(x):
    return x * jax.nn.sigmoid(x)


def _lane_tile(v, width):
    return jnp.concatenate([v] * (width // LANES), axis=1)


def _row_stats(x):
    mu = jnp.mean(x, axis=-1, keepdims=True)
    xc = x - mu
    var = jnp.mean(xc * xc, axis=-1, keepdims=True)
    return mu, xc, lax.rsqrt(var + LN_EPS)


def _ln_rows(x, g, b):
    _, xc, rstd = _row_stats(x)
    return xc * rstd * g + b


def _ln_stats_kernel(x_ref, g_ref, b_ref, hb_ref, mu_ref, rstd_ref):
    g, b = g_ref[...], b_ref[...]
    for r in range(0, x_ref.shape[0], SUBLANES):
        rows = slice(r, r + SUBLANES)
        mu, xc, rstd = _row_stats(x_ref[rows, :])
        hb_ref[rows, :] = (xc * rstd * g + b).astype(BF16)
        mu_ref[rows, :] = jnp.broadcast_to(mu, (SUBLANES, LANES))
        rstd_ref[rows, :] = jnp.broadcast_to(rstd, (SUBLANES, LANES))


def _ln_final_kernel(x_ref, g_ref, b_ref, o_ref):
    g, b = g_ref[...], b_ref[...]
    for r in range(0, x_ref.shape[0], SUBLANES):
        rows = slice(r, r + SUBLANES)
        o_ref[rows, :] = _ln_rows(x_ref[rows, :], g, b)


LN_INPUT_BUFFERS = 3


def _ln_stats_pipeline(x_hbm, g_hbm, b_hbm, hb_hbm, mu_hbm, rstd_hbm, *, tm):
    m, d = x_hbm.shape
    row = pl.BlockSpec((tm, d), lambda i: (i, 0))
    vec = pl.BlockSpec((1, d), lambda i: (0, 0))
    stat = pl.BlockSpec((tm, LANES), lambda i: (i, 0))
    pltpu.emit_pipeline(
        _ln_stats_kernel,
        grid=(m // tm,),
        in_specs=[pl.BlockSpec((tm, d), lambda i: (i, 0), pipeline_mode=pl.Buffered(LN_INPUT_BUFFERS)),
                  vec, vec],
        out_specs=[row, stat, stat],
    )(x_hbm, g_hbm, b_hbm, hb_hbm, mu_hbm, rstd_hbm)


def _layer_norm(x, g, b, *, final, tm=512):
    m, d = x.shape
    assert m % tm == 0
    if not final:
        any_spec = pl.BlockSpec(memory_space=pl.ANY)
        return pl.pallas_call(
            functools.partial(_ln_stats_pipeline, tm=tm),
            in_specs=[any_spec] * 3, out_specs=[any_spec] * 3,
            out_shape=[jax.ShapeDtypeStruct((m, d), BF16), jax.ShapeDtypeStruct((m, LANES), F32),
                       jax.ShapeDtypeStruct((m, LANES), F32)],
            compiler_params=pltpu.CompilerParams(vmem_limit_bytes=56 * MIB),
            name="layer_norm",
        )(x, g.reshape(1, d), b.reshape(1, d))
    row = pl.BlockSpec((tm, d), lambda i: (i, 0))
    vec = pl.BlockSpec((1, d), lambda i: (0, 0))
    return pl.pallas_call(
        _ln_final_kernel, grid=(m // tm,), in_specs=[row, vec, vec], out_specs=row,
        out_shape=jax.ShapeDtypeStruct((m, d), F32),
        compiler_params=_cparams(("arbitrary",), 48),
        name="layer_norm_final",
    )(x, g.reshape(1, d), b.reshape(1, d))


IN_PROJ_SCALED = ((OFF_Q, OFF_K, Q_DIFF_SCALE), (OFF_QM, OFF_GM, MEM_HEAD_DIM ** -0.5))


def _matmul_kernel(a_ref, w_ref, o_ref, *, col_axis, col_scales):
    acc = jnp.dot(a_ref[...].astype(BF16), w_ref[...].astype(BF16), preferred_element_type=F32)
    if col_scales:
        col = pl.program_id(col_axis) * o_ref.shape[1]
        scale = jnp.float32(1.0)
        for lo, hi, s in col_scales:
            scale = jnp.where((col >= lo) & (col < hi), jnp.float32(s), scale)
        acc = acc * scale
    o_ref[...] = acc.astype(o_ref.dtype)


def _in_proj(h, w_in, layer, *, tm=2048, tn=512):
    m, k = h.shape
    n = w_in.shape[-1]
    assert all(lo % tn == 0 and hi % tn == 0 for lo, hi, _ in IN_PROJ_SCALED)
    return pl.pallas_call(
        functools.partial(_matmul_kernel, col_axis=1, col_scales=IN_PROJ_SCALED),
        grid=(m // tm, n // tn),
        in_specs=[pl.BlockSpec((tm, k), lambda i, j: (i, 0)),
                  pl.BlockSpec((None, k, tn), lambda i, j: (layer, 0, j))],
        out_specs=pl.BlockSpec((tm, tn), lambda i, j: (i, j)),
        out_shape=jax.ShapeDtypeStruct((m, n), BF16),
        compiler_params=_cparams(("parallel", "parallel"), 58),
        name="in_proj",
    )(h, w_in)


_NT_DIMS = (((1,), (1,)), ((), ()))


def _diff_attn_kernel(q_ref, k_ref, v_ref, g_ref, lq1_ref, lk1_ref, lq2_ref, lk2_ref, sg_ref,
                      o_ref, s_ref, *, lam_init, seq):
    d = DIFF_HEAD_DIM
    lam = (jnp.exp(jnp.sum(lq1_ref[...] * lk1_ref[...], axis=-1, keepdims=True))
           - jnp.exp(jnp.sum(lq2_ref[...] * lk2_ref[...], axis=-1, keepdims=True))
           + lam_init)
    row_chunk = lax.broadcasted_iota(jnp.int32, (ATTN_TQ, ATTN_TQ), 0) // CHUNK
    col_chunk = lax.broadcasted_iota(jnp.int32, (ATTN_TQ, ATTN_TQ), 1) // CHUNK
    mask = col_chunk <= row_chunk
    out_gain = sg_ref[...] * (1.0 - lam_init)
    row_max = {}

    def scores(t):
        lo = t * ATTN_TQ
        for m in range(2):
            cols = slice(m * d, (m + 1) * d)
            q = q_ref[lo:lo + ATTN_TQ, cols]
            s_d = lax.dot_general(q, k_ref[lo:lo + ATTN_TQ, cols], _NT_DIMS, preferred_element_type=F32)
            s_d = jnp.where(mask, s_d, -jnp.inf)
            s_ref[t % 2, m, :, lo:lo + ATTN_TQ] = s_d
            mx = jnp.max(s_d, axis=-1, keepdims=True)
            if lo:
                s_o = lax.dot_general(q, k_ref[0:lo, cols], _NT_DIMS, preferred_element_type=F32)
                s_ref[t % 2, m, :, 0:lo] = s_o
                mx = jnp.maximum(mx, jnp.max(s_o, axis=-1, keepdims=True))
            row_max[(t, m)] = mx

    def outputs(t):
        lo = t * ATTN_TQ
        kv = lo + ATTN_TQ
        p = []
        for m in range(2):
            e = jnp.exp2(s_ref[t % 2, m, :, 0:kv] - row_max[(t, m)])
            l = jnp.sum(e, axis=-1, keepdims=True)
            p.append(e * ((lam if m else 1.0) / l))
        o = jnp.dot((p[0] - p[1]).astype(BF16), v_ref[0:kv, :], preferred_element_type=F32)
        o = o * lax.rsqrt(jnp.mean(o * o, axis=-1, keepdims=True) + RMS_EPS) * out_gain
        o_ref[lo:lo + ATTN_TQ, :] = (o * _silu(g_ref[lo:lo + ATTN_TQ, :].astype(F32))).astype(o_ref.dtype)

    n_tiles = seq // ATTN_TQ
    for t in range(n_tiles, -1, -1):
        if t >= 1:
            scores(t - 1)
        if t < n_tiles:
            outputs(t)


def _diff_attention(proj, lq1, lk1, lq2, lk2, subln_g, layer, *, batch, seq, lam_init):
    w = DIFF_V_DIM
    head_block = lambda off: pl.BlockSpec((seq, w), lambda b, h: (b, off // w + h))
    lam_spec = pl.BlockSpec((None, 1, DIFF_HEAD_DIM), lambda b, h: (layer, 0, 0))
    return pl.pallas_call(
        functools.partial(_diff_attn_kernel, lam_init=lam_init, seq=seq),
        grid=(batch, DIFF_HEADS),
        in_specs=[head_block(OFF_Q), head_block(OFF_K), head_block(OFF_V), head_block(OFF_GD),
                  lam_spec, lam_spec, lam_spec, lam_spec,
                  pl.BlockSpec((None, 1, DIFF_V_DIM), lambda b, h: (layer, 0, 0))],
        out_specs=pl.BlockSpec((seq, w), lambda b, h: (b, h)),
        out_shape=jax.ShapeDtypeStruct((batch * seq, DIFF_WIDTH), BF16),
        scratch_shapes=[pltpu.VMEM((2, 2, ATTN_TQ, seq), F32)],
        compiler_params=_cparams(("arbitrary", "arbitrary"), 48),
        name="diff_attention",
    )(proj, proj, proj, proj,
      lq1.reshape(DEPTH, 1, -1), lk1.reshape(DEPTH, 1, -1), lq2.reshape(DEPTH, 1, -1),
      lk2.reshape(DEPTH, 1, -1), subln_g.reshape(DEPTH, 1, -1))


CONV_TS = 512
CONV_RC = 64
CONV_PAD = CONV_HALO - (CONV_K - 1)


def _conv_kernel(a_ref, gt_ref, ah_ref, gh_ref, gc_ref, cw_ref, cb_ref, lg_ref, lb_ref, wpw_ref,
                 o_ref, hbuf_ref, shift_ref, cout_ref, wtap_ref, wpw_bf_ref, *, tiles_per_seq):
    i = pl.program_id(0)

    @pl.when(i == 0)
    def _():
        wpw_bf_ref[...] = wpw_ref[...].astype(BF16)
        for j in range(CONV_K):
            wtap_ref[j * SUBLANES:(j + 1) * SUBLANES, :] = jnp.broadcast_to(
                cw_ref[j:j + 1, :], (SUBLANES, CONV_WIDTH))

    halo = ah_ref[...].astype(F32) * jax.nn.sigmoid(gh_ref[...].astype(F32))
    keep = (i % tiles_per_seq != 0).astype(F32)
    hbuf_ref[0:CONV_HALO, :] = halo * keep
    hbuf_ref[CONV_HALO:CONV_HALO + CONV_TS, :] = (
        a_ref[...].astype(F32) * jax.nn.sigmoid(gt_ref[...].astype(F32)))

    n_shifted = CONV_HALO + CONV_TS - SUBLANES
    for k in range(1, SUBLANES):
        shift_ref[k - 1, 0:n_shifted, :] = hbuf_ref[k:k + n_shifted, :]
    n_sub = CONV_RC // SUBLANES
    for r in range(CONV_TS // CONV_RC):
        acc = [jnp.zeros((SUBLANES, CONV_WIDTH), F32) for _ in range(n_sub)]
        for j in range(CONV_K):
            k, base = (CONV_PAD + j) % SUBLANES, (CONV_PAD + j) // SUBLANES * SUBLANES
            wj = wtap_ref[j * SUBLANES:(j + 1) * SUBLANES, :]
            for s in range(n_sub):
                row = base + r * CONV_RC + s * SUBLANES
                win = shift_ref[k - 1, row:row + SUBLANES, :] if k else hbuf_ref[row:row + SUBLANES, :]
                acc[s] = acc[s] + wj * win
        for s in range(n_sub):
            row = r * CONV_RC + s * SUBLANES
            cout_ref[row:row + SUBLANES, :] = acc[s]

    c = cout_ref[...] + cb_ref[...]
    c = _silu(_ln_rows(c, lg_ref[...], lb_ref[...]))
    y = jnp.dot(c.astype(BF16), wpw_bf_ref[...], preferred_element_type=F32)
    o_ref[...] = (y * _silu(gc_ref[...].astype(F32))).astype(o_ref.dtype)


def _conv_branch(proj, conv_w, conv_b, ln_g, ln_b, w_pw, layer, *, seq):
    m = proj.shape[0]
    cw = CONV_WIDTH
    halo_per_tile = CONV_TS // CONV_HALO
    main = lambda off: pl.BlockSpec((CONV_TS, cw), lambda i: (i, off // cw))
    halo = lambda off: pl.BlockSpec(
        (CONV_HALO, cw), lambda i: (jnp.maximum(i * halo_per_tile - 1, 0), off // cw))
    vec = pl.BlockSpec((None, 1, cw), lambda i: (layer, 0, 0))
    return pl.pallas_call(
        functools.partial(_conv_kernel, tiles_per_seq=seq // CONV_TS),
        grid=(m // CONV_TS,),
        in_specs=[main(OFF_GLU_A), main(OFF_GLU_G), halo(OFF_GLU_A), halo(OFF_GLU_G), main(OFF_GC),
                  pl.BlockSpec((None, CONV_K, cw), lambda i: (layer, 0, 0)),
                  vec, vec, vec,
                  pl.BlockSpec((None, cw, cw), lambda i: (layer, 0, 0))],
        out_specs=pl.BlockSpec((CONV_TS, cw), lambda i: (i, 0)),
        out_shape=jax.ShapeDtypeStruct((m, cw), BF16),
        scratch_shapes=[pltpu.VMEM((CONV_HALO + CONV_TS, cw), F32),
                        pltpu.VMEM((SUBLANES - 1, CONV_HALO + CONV_TS, cw), F32),
                        pltpu.VMEM((CONV_TS, cw), F32),
                        pltpu.VMEM((CONV_K * SUBLANES, cw), F32),
                        pltpu.VMEM((cw, cw), BF16)],
        compiler_params=_cparams(("arbitrary",), 48),
        name="conv_branch",
    )(proj, proj, proj, proj, proj, conv_w, conv_b.reshape(DEPTH, 1, cw),
      ln_g.reshape(DEPTH, 1, cw), ln_b.reshape(DEPTH, 1, cw), w_pw)


MEM_TQ = 2048


def _mem_attn_kernel(q_ref, k_ref, v_ref, g_ref, o_ref):
    for h in range(MEM_HEADS):
        cols = slice(h * MEM_HEAD_DIM, (h + 1) * MEM_HEAD_DIM)
        s = lax.dot_general(q_ref[:, cols], k_ref[:, cols], _NT_DIMS, preferred_element_type=F32)
        p = jnp.exp(s - jnp.max(s, axis=-1, keepdims=True))
        l = jnp.sum(p, axis=-1, keepdims=True)
        o = jnp.dot(p.astype(BF16), v_ref[:, cols], preferred_element_type=F32) * (1.0 / l)
        o_ref[:, cols] = (o * _silu(g_ref[:, cols].astype(F32))).astype(o_ref.dtype)


def _mem_kv(mem, w_mem_kv, *, tn=512):
    m, k = mem.shape
    n = w_mem_kv.shape[-1]
    per_layer = n // tn
    return pl.pallas_call(
        functools.partial(_matmul_kernel, col_axis=0, col_scales=()),
        grid=(DEPTH * per_layer,),
        in_specs=[pl.BlockSpec((m, k), lambda j: (0, 0), pipeline_mode=pl.Buffered(1)),
                  pl.BlockSpec((None, k, tn), lambda j: (j // per_layer, 0, j % per_layer))],
        out_specs=pl.BlockSpec((m, tn), lambda j: (0, j)),
        out_shape=jax.ShapeDtypeStruct((m, DEPTH * n), BF16),
        compiler_params=_cparams(("arbitrary",), 52),
        name="mem_kv",
    )(mem, w_mem_kv)


def _mem_attention(proj, kv, layer, *, batch, seq, n_mem):
    w = MEM_WIDTH
    tq = seq // MEM_TQ
    return pl.pallas_call(
        _mem_attn_kernel,
        grid=(batch, tq),
        in_specs=[pl.BlockSpec((MEM_TQ, w), lambda b, t: (b * tq + t, OFF_QM // w)),
                  pl.BlockSpec((n_mem, w), lambda b, t: (b, 2 * layer)),
                  pl.BlockSpec((n_mem, w), lambda b, t: (b, 2 * layer + 1)),
                  pl.BlockSpec((MEM_TQ, w), lambda b, t: (b * tq + t, OFF_GM // w))],
        out_specs=pl.BlockSpec((MEM_TQ, w), lambda b, t: (b * tq + t, 0)),
        out_shape=jax.ShapeDtypeStruct((batch * seq, MEM_WIDTH), BF16),
        compiler_params=_cparams(("arbitrary", "arbitrary"), 32),
        name="mem_attention",
    )(proj, kv, kv, proj)


def _out_proj_kernel(yd_ref, yc_ref, ym_ref, w_ref, zp_ref, mu_ref, rstd_ref, g_ref, b_ref, z_ref):
    c0, c1 = DIFF_WIDTH, DIFF_WIDTH + CONV_WIDTH
    acc = jnp.dot(yd_ref[...], w_ref[0:c0, :].astype(BF16), preferred_element_type=F32)
    acc = acc + jnp.dot(yc_ref[...], w_ref[c0:c1, :].astype(BF16), preferred_element_type=F32)
    acc = acc + jnp.dot(ym_ref[...], w_ref[c1:D_MODEL, :].astype(BF16), preferred_element_type=F32)
    tn = z_ref.shape[1]
    h = ((zp_ref[...] - _lane_tile(mu_ref[...], tn)) * _lane_tile(rstd_ref[...], tn) * g_ref[...]
         + b_ref[...])
    z_ref[...] = DEEPNORM_ALPHA * h + acc


def _out_proj(yd, yc, ym, w_out, layer, z_prev, mu, rstd, g_prev, b_prev, *, tm=512, tn=1024):
    m = z_prev.shape[0]
    piece = pl.BlockSpec((tm, tn), lambda j, i: (i, j))
    stat = pl.BlockSpec((tm, LANES), lambda j, i: (i, 0))
    vec = pl.BlockSpec((1, tn), lambda j, i: (0, j))
    return pl.pallas_call(
        _out_proj_kernel,
        grid=(D_MODEL // tn, m // tm),
        in_specs=[pl.BlockSpec((tm, DIFF_WIDTH), lambda j, i: (i, 0)),
                  pl.BlockSpec((tm, CONV_WIDTH), lambda j, i: (i, 0)),
                  pl.BlockSpec((tm, MEM_WIDTH), lambda j, i: (i, 0)),
                  pl.BlockSpec((None, D_MODEL, tn), lambda j, i: (layer, 0, j)),
                  piece, stat, stat, vec, vec],
        out_specs=piece,
        out_shape=jax.ShapeDtypeStruct((m, D_MODEL), F32),
        compiler_params=_cparams(("parallel", "parallel"), 56),
        name="out_proj",
    )(yd, yc, ym, w_out, z_prev, mu, rstd, g_prev.reshape(1, D_MODEL), b_prev.reshape(1, D_MODEL))


def kernel(x, mem, ln_in_g, ln_in_b, w_in, lambda_q1, lambda_k1, lambda_q2, lambda_k2, subln_g,
           conv_w, conv_b, conv_ln_g, conv_ln_b, w_conv_pw, w_mem_kv, w_out, ln_post_g, ln_post_b):
    batch, seq, d = x.shape
    n_mem = mem.shape[1]
    assert d == D_MODEL and seq % ATTN_TQ == 0 and seq % CONV_TS == 0 and seq % MEM_TQ == 0
    z = x.reshape(batch * seq, d)
    ln_g, ln_b = ln_in_g, ln_in_b
    kv = _mem_kv(mem.reshape(batch * n_mem, d), w_mem_kv)
    for layer in range(DEPTH):
        lam_init = 0.8 - 0.6 * math.exp(-0.3 * layer)
        h_bf16, mu, rstd = _layer_norm(z, ln_g, ln_b, final=False)
        proj = _in_proj(h_bf16, w_in, layer)
        yd = _diff_attention(proj, lambda_q1, lambda_k1, lambda_q2, lambda_k2, subln_g, layer,
                             batch=batch, seq=seq, lam_init=lam_init)
        yc = _conv_branch(proj, conv_w, conv_b, conv_ln_g, conv_ln_b, w_conv_pw, layer, seq=seq)
        ym = _mem_attention(proj, kv, layer, batch=batch, seq=seq, n_mem=n_mem)
        z = _out_proj(yd, yc, ym, w_out, layer, z, mu, rstd, ln_g, ln_b)
        ln_g, ln_b = ln_post_g[layer], ln_post_b[layer]
    return _layer_norm(z, ln_g, ln_b, final=True).reshape(batch, seq, d)
```
